```python
import jax, jax.numpy as jnp
from jax import lax
import numpy as np

D_MODEL = 2048
BATCH = 8
SEQ = 4096
DEPTH = 4

N_META = 16
D_MIX = D_MODEL
RWKV_WIDTH = D_MIX // 2
RWKV_HEAD = 64
RWKV_HEADS = RWKV_WIDTH // RWKV_HEAD
FOX_WIDTH = D_MIX - RWKV_WIDTH
FOX_HEAD = 128
FOX_HEADS = FOX_WIDTH // FOX_HEAD
W_LORA = 64
A_LORA = 64
G_LORA = 160
RWKV_COLS = 3 * RWKV_WIDTH + W_LORA + A_LORA + G_LORA
FOX_COLS = 3 * FOX_WIDTH + FOX_HEADS
D_IN = RWKV_COLS + FOX_COLS
D_FF = ((8 * D_MODEL // 3 + 255) // 256) * 256
Q_BLOCK = 128
NORM_EPS = 1e-6
LNX_EPS = 64e-5
L2_EPS = 1e-12
NEG_INF = -1e30

kernel_name = "hymba_rwkv7_fox_macaron_trunk"


def rmsnorm(x, g):
    xf = x.astype(jnp.float32)
    y = xf * lax.rsqrt(jnp.mean(xf * xf, axis=-1, keepdims=True) + NORM_EPS)
    return (y * g.astype(jnp.float32)).astype(x.dtype)


def swiglu_ffn(x, w_gu, w_down):
    gate, up = jnp.split(x @ w_gu, 2, axis=-1)
    return (jax.nn.silu(gate) * up) @ w_down


def token_shift(p, mu):
    prev = jnp.pad(p, ((0, 0), (1, 0), (0, 0)))[:, :-1]
    return p + (prev - p) * mu


def rwkv7_scan(r, w, k, v, a, b):
    B, L, H, N = r.shape

    def step(S, inp):
        r_t, w_t, k_t, v_t, a_t, b_t = inp
        sa = jnp.einsum('bhvk,bhk->bhv', S, a_t)
        S = (S * w_t[:, :, None, :] + sa[..., None] * b_t[:, :, None, :]
             + v_t[..., None] * k_t[:, :, None, :])
        return S, jnp.einsum('bhvk,bhk->bhv', S, r_t)

    xs = tuple(jnp.swapaxes(t, 0, 1) for t in (r, w, k, v, a, b))
    S0 = jnp.zeros((B, H, N, N), jnp.float32)
    _, ys = lax.scan(step, S0, xs)
    return jnp.swapaxes(ys, 0, 1)


def rwkv7_group(p, mu, w0, w_up, a0, a_up, g_up, k_k, k_a, r_k, lnx_w, lnx_b):
    B, L, _ = p.shape
    pf = token_shift(p.astype(jnp.float32), mu)
    c1 = RWKV_WIDTH
    c2, c3 = 2 * c1, 3 * c1
    c4 = c3 + W_LORA
    c5 = c4 + A_LORA
    r, k, v = pf[..., :c1], pf[..., c1:c2], pf[..., c2:c3]
    wd, ad, gd = pf[..., c3:c4], pf[..., c4:c5], pf[..., c5:]
    w_log = -jax.nn.softplus(-(w0 + jnp.tanh(wd) @ w_up)) - 0.5
    decay = jnp.exp(-jnp.exp(w_log))
    a = jax.nn.sigmoid(a0 + ad @ a_up)
    g = jax.nn.sigmoid(gd) @ g_up

    def heads(t):
        return t.reshape(B, L, RWKV_HEADS, RWKV_HEAD)

    kk = heads(k * k_k)
    kk = kk / jnp.maximum(jnp.sqrt(jnp.sum(kk * kk, axis=-1, keepdims=True)), L2_EPS)
    k = k * (1.0 + (a - 1.0) * k_a)
    rh, kh, vh, ah = heads(r), heads(k), heads(v), heads(a)
    y = rwkv7_scan(rh, heads(decay), kh, vh, -kk, kk * ah)
    mean = jnp.mean(y, axis=-1, keepdims=True)
    var = jnp.mean(jnp.square(y - mean), axis=-1, keepdims=True)
    y = ((y - mean) * lax.rsqrt(var + LNX_EPS)).reshape(B, L, RWKV_WIDTH) * lnx_w + lnx_b
    bonus = jnp.sum(rh * kh * r_k, axis=-1, keepdims=True) * vh
    out = (y + bonus.reshape(B, L, RWKV_WIDTH)) * g
    return out.astype(p.dtype)


def fox_group(p, b_f, out_gain):
    B, L, _ = p.shape
    W = FOX_WIDTH

    def heads(t):
        return jnp.transpose(t.reshape(B, L, FOX_HEADS, FOX_HEAD), (0, 2, 1, 3))

    q, k, v = heads(p[..., :W]), heads(p[..., W:2 * W]), heads(p[..., 2 * W:3 * W])
    log_f = jax.nn.log_sigmoid((p[..., 3 * W:] + b_f).astype(jnp.float32))
    c = jnp.transpose(jnp.cumsum(log_f, axis=1), (0, 2, 1))
    n_blocks = -(-L // Q_BLOCK)
    Lp = n_blocks * Q_BLOCK
    pad4 = ((0, 0), (0, 0), (0, Lp - L), (0, 0))
    q, k, v = [jnp.pad(t, pad4) for t in (q, k, v)]
    c = jnp.pad(c, ((0, 0), (0, 0), (0, Lp - L)))
    pos = jnp.arange(Lp)
    scale = FOX_HEAD ** -0.5
    outs = []
    for i in range(n_blocks):
        q0, q1 = i * Q_BLOCK, (i + 1) * Q_BLOCK
        s = jnp.einsum('bhqd,bhkd->bhqk', q[:, :, q0:q1], k[:, :, :q1]).astype(jnp.float32) * scale
        s = s + c[:, :, q0:q1, None] - c[:, :, None, :q1]
        mask = pos[None, :q1] <= pos[q0:q1, None]
        pr = jax.nn.softmax(jnp.where(mask, s, NEG_INF), axis=-1)
        outs.append(jnp.einsum('bhqk,bhkd->bhqd', pr.astype(v.dtype), v[:, :, :q1]))
    o = jnp.concatenate(outs, axis=2)[:, :, :L]
    o = jnp.transpose(o, (0, 2, 1, 3)).reshape(B, L, W)
    return rmsnorm(o, out_gain)


def setup_inputs(seed: int = 0) -> dict:
    key = jax.random.key(seed)
    ks = jax.random.split(key, 26)
    f32 = jnp.float32

    def nrm(k, shape, s):
        return jax.random.normal(k, shape, f32) * s

    def gain(k, shape):
        return 1.0 + 0.02 * jax.random.normal(k, shape, f32)

    return {
        "x": nrm(ks[0], (BATCH, SEQ, D_MODEL), 1.0),
        "meta_tokens": nrm(ks[1], (N_META, D_MODEL), 1.0),
        "ffn1_norm": gain(ks[2], (DEPTH, D_MODEL)),
        "ffn1_w_gu": nrm(ks[3], (DEPTH, D_MODEL, 2 * D_FF), D_MODEL ** -0.5),
        "ffn1_w_down": nrm(ks[4], (DEPTH, D_FF, D_MODEL), D_FF ** -0.5),
        "mix_norm": gain(ks[5], (DEPTH, D_MODEL)),
        "w_in": nrm(ks[6], (DEPTH, D_MODEL, D_IN), D_MODEL ** -0.5),
        "rwkv_mu": jax.random.uniform(ks[7], (DEPTH, RWKV_COLS), f32),
        "rwkv_w0": jax.random.uniform(ks[8], (DEPTH, RWKV_WIDTH), f32, -2.5, 0.5),
        "rwkv_w_up": nrm(ks[9], (DEPTH, W_LORA, RWKV_WIDTH), 0.5 * W_LORA ** -0.5),
        "rwkv_a0": nrm(ks[10], (DEPTH, RWKV_WIDTH), 0.1),
        "rwkv_a_up": nrm(ks[11], (DEPTH, A_LORA, RWKV_WIDTH), 0.5 * A_LORA ** -0.5),
        "rwkv_g_up": nrm(ks[12], (DEPTH, G_LORA, RWKV_WIDTH), G_LORA ** -0.5),
        "rwkv_k_k": 0.85 + nrm(ks[13], (DEPTH, RWKV_WIDTH), 0.05),
        "rwkv_k_a": 1.0 + nrm(ks[14], (DEPTH, RWKV_WIDTH), 0.05),
        "rwkv_r_k": nrm(ks[15], (DEPTH, RWKV_HEADS, RWKV_HEAD), 0.1),
        "rwkv_lnx_w": gain(ks[16], (DEPTH, RWKV_WIDTH)),
        "rwkv_lnx_b": nrm(ks[17], (DEPTH, RWKV_WIDTH), 0.02),
        "fox_b_f": jax.random.uniform(ks[18], (DEPTH, FOX_HEADS), f32, 1.0, 5.0),
        "fox_out_norm": gain(ks[19], (DEPTH, FOX_WIDTH)),
        "w_out": nrm(ks[20], (DEPTH, D_MIX, D_MODEL), D_MIX ** -0.5),
        "ffn2_norm": gain(ks[21], (DEPTH, D_MODEL)),
        "ffn2_w_gu": nrm(ks[22], (DEPTH, D_MODEL, 2 * D_FF), D_MODEL ** -0.5),
        "ffn2_w_down": nrm(ks[23], (DEPTH, D_FF, D_MODEL), D_FF ** -0.5),
        "final_norm": gain(ks[24], (D_MODEL,)),
    }


def reference(x, meta_tokens, ffn1_norm, ffn1_w_gu, ffn1_w_down, mix_norm, w_in,
              rwkv_mu, rwkv_w0, rwkv_w_up, rwkv_a0, rwkv_a_up, rwkv_g_up, rwkv_k_k,
              rwkv_k_a, rwkv_r_k, rwkv_lnx_w, rwkv_lnx_b, fox_b_f, fox_out_norm, w_out,
              ffn2_norm, ffn2_w_gu, ffn2_w_down, final_norm):
    B = x.shape[0]
    meta = jnp.broadcast_to(meta_tokens.astype(x.dtype)[None], (B, N_META, D_MODEL))
    h = jnp.concatenate([meta, x], axis=1)
    for l in range(DEPTH):
        h = h + 0.5 * swiglu_ffn(rmsnorm(h, ffn1_norm[l]), ffn1_w_gu[l], ffn1_w_down[l])
        p = rmsnorm(h, mix_norm[l]) @ w_in[l]
        y_rwkv = rwkv7_group(p[..., :RWKV_COLS], rwkv_mu[l], rwkv_w0[l], rwkv_w_up[l],
                             rwkv_a0[l], rwkv_a_up[l], rwkv_g_up[l], rwkv_k_k[l],
                             rwkv_k_a[l], rwkv_r_k[l], rwkv_lnx_w[l], rwkv_lnx_b[l])
        y_fox = fox_group(p[..., RWKV_COLS:], fox_b_f[l], fox_out_norm[l])
        h = h + jnp.concatenate([y_rwkv, y_fox], axis=-1) @ w_out[l]
        h = h + 0.5 * swiglu_ffn(rmsnorm(h, ffn2_norm[l]), ffn2_w_gu[l], ffn2_w_down[l])
    return rmsnorm(h, final_norm)[:, N_META:]
```

```python
import functools

import jax
import jax.numpy as jnp
from jax import lax
from jax.experimental import pallas as pl
from jax.experimental.pallas import tpu as pltpu

F32 = jnp.float32
BF16 = jnp.bfloat16

N_META = 16
RWKV_HEAD = 64
FOX_HEAD = 128
W_LORA = 64
A_LORA = 64
G_LORA = 160
NORM_EPS = 1e-6
LNX_EPS = 64e-5
L2_EPS = 1e-12
NEG_INF = -1e30

LANES = 128
VMEM_LIMIT_BYTES = 56 * 1024 * 1024

CHUNK = 64
PAIR = 2 * RWKV_HEAD
TIME_ALIGN = 384
ATT_TQ = 384
ATT_TK = 384
GATE_TB = 384
FFN_TM = 512
FFN_TF = 512
PROJ_TM = 1024
PROJ_TN = 512
OUT_TM = 512
LORA_W = 512
F_LANE = 448


def _rms_scale(x, gain):
    ms = jnp.mean(x * x, axis=-1, keepdims=True)
    return x * lax.rsqrt(ms + NORM_EPS) * gain


def _dot(a, b):
    return jnp.dot(a.astype(BF16), b.astype(BF16), preferred_element_type=F32)


def _dot_nt(a, b):
    return lax.dot_general(a.astype(BF16), b.astype(BF16), (((1,), (1,)), ((), ())),
                           preferred_element_type=F32)


def _split3(x):
    hi = x.astype(BF16)
    r1 = x - hi.astype(F32)
    mid = r1.astype(BF16)
    lo = (r1 - mid.astype(F32)).astype(BF16)
    return hi, mid, lo


def _dot_exact_lhs(m_bf16, x):
    hi, mid, lo = _split3(x)
    acc = jnp.dot(m_bf16, lo, preferred_element_type=F32)
    acc = acc + jnp.dot(m_bf16, mid, preferred_element_type=F32)
    return acc + jnp.dot(m_bf16, hi, preferred_element_type=F32)


def _dot_exact_rhs(x, m_bf16):
    hi, mid, lo = _split3(x)
    acc = jnp.dot(lo, m_bf16, preferred_element_type=F32)
    acc = acc + jnp.dot(mid, m_bf16, preferred_element_type=F32)
    return acc + jnp.dot(hi, m_bf16, preferred_element_type=F32)


def _params(*sem):
    return pltpu.CompilerParams(dimension_semantics=sem, vmem_limit_bytes=VMEM_LIMIT_BYTES)


def _ffn_body(h_ref, g_ref, wg_ref, wu_ref, wd_ref, o_ref, u_ref):
    f = pl.program_id(1)

    @pl.when(f == 0)
    def _():
        x = h_ref[...]
        u_ref[...] = _rms_scale(x, g_ref[...]).astype(BF16)
        o_ref[...] = x

    u = u_ref[...]
    gate = jnp.dot(u, wg_ref[...], preferred_element_type=F32)
    up = jnp.dot(u, wu_ref[...], preferred_element_type=F32)
    act = (0.5 * gate) * jax.nn.sigmoid(gate) * up
    o_ref[...] += jnp.dot(act.astype(BF16), wd_ref[...], preferred_element_type=F32)


def _ffn(h, gain, w_gu, w_down):
    t, d = h.shape
    ff = w_down.shape[0]
    nf = ff // FFN_TF
    return pl.pallas_call(
        _ffn_body,
        grid=(t // FFN_TM, nf),
        in_specs=[
            pl.BlockSpec((FFN_TM, d), lambda i, f: (i, 0)),
            pl.BlockSpec((1, d), lambda i, f: (0, 0)),
            pl.BlockSpec((d, FFN_TF), lambda i, f: (0, f)),
            pl.BlockSpec((d, FFN_TF), lambda i, f: (0, f + nf)),
            pl.BlockSpec((FFN_TF, d), lambda i, f: (f, 0)),
        ],
        out_specs=pl.BlockSpec((FFN_TM, d), lambda i, f: (i, 0)),
        out_shape=jax.ShapeDtypeStruct((t, d), F32),
        scratch_shapes=[pltpu.VMEM((FFN_TM, d), BF16)],
        compiler_params=_params("parallel", "arbitrary"),
    )(h, gain, w_gu, w_gu, w_down)


def _proj_body(h_ref, g_ref, w_ref, o_ref, u_ref):
    @pl.when(pl.program_id(1) == 0)
    def _():
        u_ref[...] = _rms_scale(h_ref[...], g_ref[...]).astype(BF16)

    o_ref[...] = jnp.dot(u_ref[...], w_ref[...], preferred_element_type=F32)


def _in_proj(h, gain, w):
    t, d = h.shape
    n = w.shape[1]
    return pl.pallas_call(
        _proj_body,
        grid=(t // PROJ_TM, n // PROJ_TN),
        in_specs=[
            pl.BlockSpec((PROJ_TM, d), lambda i, j: (i, 0)),
            pl.BlockSpec((1, d), lambda i, j: (0, 0)),
            pl.BlockSpec((d, PROJ_TN), lambda i, j: (0, j)),
        ],
        out_specs=pl.BlockSpec((PROJ_TM, PROJ_TN), lambda i, j: (i, j)),
        out_shape=jax.ShapeDtypeStruct((t, n), F32),
        scratch_shapes=[pltpu.VMEM((PROJ_TM, d), BF16)],
        compiler_params=_params("parallel", "arbitrary"),
    )(h, gain, w)


def _rwkv_body(rkv_ref, lora_ref, mu_rkv_ref, mu_lora_ref, w0_ref, wup_ref, a0_ref, aup_ref,
               gup_ref, kk_ref, ka_ref, rk_ref, lnw_ref, lnb_ref, o_ref,
               carry_rkv, carry_lora, st_ref, *, width):
    c = pl.program_id(1)

    @pl.when(c == 0)
    def _():
        carry_rkv[...] = jnp.zeros_like(carry_rkv)
        carry_lora[...] = jnp.zeros_like(carry_lora)
        st_ref[...] = jnp.zeros_like(st_ref)

    row = lax.broadcasted_iota(jnp.int32, (CHUNK, 1), 0)

    def shift(p, carry_ref, mu):
        prev = jnp.where(row == 0, carry_ref[...], pltpu.roll(p, 1, 0))
        carry_ref[...] = p[CHUNK - 1:CHUNK, :]
        return p + (prev - p) * mu

    pf = shift(rkv_ref[0], carry_rkv, mu_rkv_ref[...])
    lf = shift(lora_ref[0], carry_lora, mu_lora_ref[...])
    r = pf[:, :width]
    k = pf[:, width:2 * width]
    v = pf[:, 2 * width:]
    wd = lf[:, 0:LANES]
    ad = lf[:, LANES:2 * LANES]
    gd = lf[:, 2 * LANES:4 * LANES]

    z = w0_ref[...] + _dot(jnp.tanh(wd), wup_ref[...])
    nz = -z
    softplus = jnp.maximum(nz, 0.0) + jnp.log1p(jnp.exp(-jnp.abs(nz)))
    logw = -jnp.exp(-softplus - 0.5)
    a = jax.nn.sigmoid(a0_ref[...] + _dot(ad, aup_ref[...]))
    g = _dot(jax.nn.sigmoid(gd), gup_ref[...])

    ci = lax.broadcasted_iota(jnp.int32, (CHUNK, CHUNK), 0)
    cj = lax.broadcasted_iota(jnp.int32, (CHUNK, CHUNK), 1)
    ltri = (cj <= ci).astype(BF16)
    cs = _dot_exact_lhs(ltri, logw)

    ii = lax.broadcasted_iota(jnp.int32, (PAIR, PAIR), 0)
    jj = lax.broadcasted_iota(jnp.int32, (PAIR, PAIR), 1)
    same_head = (ii // RWKV_HEAD) == (jj // RWKV_HEAD)
    mask_strict = same_head & (jj < ii)
    mask_incl = same_head & (jj <= ii)
    eye = ii == jj
    seg_ones = same_head.astype(BF16)
    lane = lax.broadcasted_iota(jnp.int32, (CHUNK, PAIR), 1)
    head0 = lane < RWKV_HEAD

    def stack(x):
        return jnp.concatenate([jnp.where(head0, x, 0.0), jnp.where(head0, 0.0, x)], axis=0)

    def both(x):
        return jnp.concatenate([x, x], axis=0)

    for hp in range(width // PAIR):
        sl = slice(hp * PAIR, (hp + 1) * PAIR)
        r_p, k_p, v_p, a_p, lw, cs_p = r[:, sl], k[:, sl], v[:, sl], a[:, sl], logw[:, sl], cs[:, sl]
        kk = k_p * kk_ref[:, sl]
        ss = _dot_exact_rhs(kk * kk, seg_ones)
        kk = kk / jnp.maximum(jnp.sqrt(ss), L2_EPS)
        k2 = k_p * (1.0 + (a_p - 1.0) * ka_ref[:, sl])
        bv = kk * a_p
        last = cs_p[CHUNK - 1:CHUNK, :]
        e_neg = jnp.exp(-cs_p)
        e_end = jnp.exp(last - cs_p)
        xa = stack(-kk * jnp.exp(cs_p - lw))
        xr = stack(r_p * jnp.exp(cs_p))
        xbh = stack(bv * e_end)
        xkh = stack(k2 * e_end)
        vst = stack(v_p)

        big = _dot_nt(jnp.concatenate([xa, xr], axis=0),
                      jnp.concatenate([both(bv * e_neg), both(k2 * e_neg)], axis=0))
        a_ab = jnp.where(mask_strict, big[:PAIR, :PAIR], 0.0)
        a_ak = jnp.where(mask_strict, big[:PAIR, PAIR:], 0.0)
        a_rb = jnp.where(mask_incl, big[PAIR:, :PAIR], 0.0)
        a_rk = jnp.where(mask_incl, big[PAIR:, PAIR:], 0.0)

        tinv = jnp.where(eye | (((ii ^ jj) == 1) & (jj < ii)), jnp.where(eye, 1.0, a_ab), 0.0)
        s = 2
        while s < RWKV_HEAD:
            e_mask = ((ii // (2 * s)) == (jj // (2 * s))) & (((ii // s) % 2) == 1) & (((jj // s) % 2) == 0)
            tinv = tinv + _dot(_dot(tinv, jnp.where(e_mask, a_ab, 0.0)), tinv)
            s *= 2

        x = _dot(tinv, jnp.concatenate([xa, _dot(a_ak, vst)], axis=1))
        mg = _dot(xbh.T, x)
        m = mg[:, :PAIR] + jnp.where(eye, jnp.exp(last), 0.0)
        gg = mg[:, PAIR:] + _dot(xkh.T, vst)
        ry = _dot(a_rb, x)
        rh = xr + ry[:, :PAIR]
        y0 = ry[:, PAIR:] + _dot(a_rk, vst)
        st = st_ref[hp]
        yst = _dot(rh, st) + y0
        st_ref[hp] = _dot(m, st) + gg
        y = yst[:CHUNK] + yst[CHUNK:]

        inv_n = 1.0 / RWKV_HEAD
        mean = _dot_exact_rhs(y, seg_ones) * inv_n
        dlt = y - mean
        var = _dot_exact_rhs(dlt * dlt, seg_ones) * inv_n
        yn = dlt * lax.rsqrt(var + LNX_EPS) * lnw_ref[:, sl] + lnb_ref[:, sl]
        bonus = _dot_exact_rhs(r_p * k2 * rk_ref[:, sl], seg_ones) * v_p
        o_ref[0, :, sl] = ((yn + bonus) * g[:, sl]).astype(o_ref.dtype)


def _rwkv(p3, prm, width):
    b, lp, _ = p3.shape
    n_rkv = 3 * width
    lora_blk = (2 * n_rkv) // LORA_W
    vec = lambda n: pl.BlockSpec((1, n), lambda i, c: (0, 0))
    mat = lambda k: pl.BlockSpec((k, width), lambda i, c: (0, 0))
    return pl.pallas_call(
        functools.partial(_rwkv_body, width=width),
        grid=(b, lp // CHUNK),
        in_specs=[
            pl.BlockSpec((1, CHUNK, n_rkv), lambda i, c: (i, c, 0)),
            pl.BlockSpec((1, CHUNK, LORA_W), lambda i, c: (i, c, lora_blk)),
            vec(n_rkv), vec(LORA_W), vec(width), mat(LANES), vec(width), mat(LANES),
            mat(2 * LANES), vec(width), vec(width), vec(width), vec(width), vec(width),
        ],
        out_specs=pl.BlockSpec((1, CHUNK, width), lambda i, c: (i, c, 0)),
        out_shape=jax.ShapeDtypeStruct((b, lp, width), BF16),
        scratch_shapes=[
            pltpu.VMEM((1, n_rkv), F32),
            pltpu.VMEM((1, LORA_W), F32),
            pltpu.VMEM((width // PAIR, PAIR, PAIR), F32),
        ],
        compiler_params=_params("parallel", "arbitrary"),
    )(p3, p3, prm["mu_rkv"], prm["mu_lora"], prm["w0"], prm["w_up"], prm["a0"], prm["a_up"],
      prm["g_up"], prm["k_k"], prm["k_a"], prm["r_k"], prm["lnx_w"], prm["lnx_b"])


def _gate_body(x_ref, bf_ref, o_ref, carry_ref):
    @pl.when(pl.program_id(1) == 0)
    def _():
        carry_ref[...] = jnp.zeros_like(carry_ref)

    zz = x_ref[0] + bf_ref[...]
    log_f = jnp.minimum(zz, 0.0) - jnp.log1p(jnp.exp(-jnp.abs(zz)))
    ci = lax.broadcasted_iota(jnp.int32, (GATE_TB, GATE_TB), 0)
    cj = lax.broadcasted_iota(jnp.int32, (GATE_TB, GATE_TB), 1)
    csum = _dot_exact_lhs((cj <= ci).astype(BF16), log_f) + carry_ref[...]
    carry_ref[...] = csum[GATE_TB - 1:GATE_TB, :]
    o_ref[0] = csum


def _gate_cumsum(p3, bf_row, lane_blk):
    b, lp, _ = p3.shape
    return pl.pallas_call(
        _gate_body,
        grid=(b, lp // GATE_TB),
        in_specs=[
            pl.BlockSpec((1, GATE_TB, LANES), lambda i, c: (i, c, lane_blk)),
            pl.BlockSpec((1, LANES), lambda i, c: (0, 0)),
        ],
        out_specs=pl.BlockSpec((1, GATE_TB, LANES), lambda i, c: (i, c, 0)),
        out_shape=jax.ShapeDtypeStruct((b, lp, LANES), F32),
        scratch_shapes=[pltpu.VMEM((1, LANES), F32)],
        compiler_params=_params("parallel", "arbitrary"),
    )(p3, bf_row)


def _attn_body(q_ref, k_ref, v_ref, cq_ref, ck_ref, o_ref, m_ref, l_ref, acc_ref, *, f_lane0):
    h = pl.program_id(1)
    qi = pl.program_id(2)
    ki = pl.program_id(3)

    @pl.when(ki == 0)
    def _():
        m_ref[...] = jnp.full_like(m_ref, NEG_INF)
        l_ref[...] = jnp.zeros_like(l_ref)
        acc_ref[...] = jnp.zeros_like(acc_ref)

    @pl.when(ki <= qi)
    def _():
        lane = lax.broadcasted_iota(jnp.int32, (ATT_TQ, LANES), 1)
        cq = jnp.sum(jnp.where(lane == f_lane0 + h, cq_ref[0], 0.0), axis=-1, keepdims=True)
        s = _dot_nt(q_ref[0], k_ref[0]) * (FOX_HEAD ** -0.5)
        s = s + cq - ck_ref[0, 0]
        qpos = qi * ATT_TQ + lax.broadcasted_iota(jnp.int32, (ATT_TQ, ATT_TK), 0)
        kpos = ki * ATT_TK + lax.broadcasted_iota(jnp.int32, (ATT_TQ, ATT_TK), 1)
        s = jnp.where(kpos <= qpos, s, NEG_INF)
        m_prev = m_ref[...]
        m_new = jnp.maximum(m_prev, jnp.max(s, axis=-1, keepdims=True))
        alpha = jnp.exp(m_prev - m_new)
        p = jnp.exp(s - m_new)
        l_ref[...] = alpha * l_ref[...] + jnp.sum(p, axis=-1, keepdims=True)
        acc_ref[...] = alpha * acc_ref[...] + _dot(p, v_ref[0])
        m_ref[...] = m_new

    @pl.when(ki == qi)
    def _():
        o_ref[0] = acc_ref[...] / l_ref[...]


def _attention(p3, c_all, c_rows, width, f_lane0):
    b, lp, _ = p3.shape
    heads = width // FOX_HEAD
    base = 3 * width // FOX_HEAD
    nq = lp // ATT_TQ
    return pl.pallas_call(
        functools.partial(_attn_body, f_lane0=f_lane0),
        grid=(b, heads, nq, lp // ATT_TK),
        in_specs=[
            pl.BlockSpec((1, ATT_TQ, FOX_HEAD), lambda i, h, q, k: (i, q, base + h)),
            pl.BlockSpec((1, ATT_TK, FOX_HEAD), lambda i, h, q, k: (i, jnp.minimum(k, q), base + heads + h)),
            pl.BlockSpec((1, ATT_TK, FOX_HEAD), lambda i, h, q, k: (i, jnp.minimum(k, q), base + 2 * heads + h)),
            pl.BlockSpec((1, ATT_TQ, LANES), lambda i, h, q, k: (i, q, 0)),
            pl.BlockSpec((1, 1, 1, ATT_TK), lambda i, h, q, k: (i, h, 0, jnp.minimum(k, q))),
        ],
        out_specs=pl.BlockSpec((1, ATT_TQ, FOX_HEAD), lambda i, h, q, k: (i, q, h)),
        out_shape=jax.ShapeDtypeStruct((b, lp, width), F32),
        scratch_shapes=[
            pltpu.VMEM((ATT_TQ, 1), F32),
            pltpu.VMEM((ATT_TQ, 1), F32),
            pltpu.VMEM((ATT_TQ, FOX_HEAD), F32),
        ],
        compiler_params=_params("parallel", "parallel", "parallel", "arbitrary"),
    )(p3, p3, p3, c_all, c_rows)


def _out_body(h_ref, yr_ref, yf_ref, gn_ref, w1_ref, w2_ref, o_ref):
    yf = _rms_scale(yf_ref[...], gn_ref[...])
    acc = jnp.dot(yr_ref[...], w1_ref[...], preferred_element_type=F32)
    acc = acc + jnp.dot(yf.astype(BF16), w2_ref[...], preferred_element_type=F32)
    o_ref[...] = h_ref[...] + acc


def _out_proj(h, y_rwkv, y_fox, fox_gain, w_out):
    t, d = h.shape
    w1 = y_rwkv.shape[1]
    w2 = y_fox.shape[1]
    return pl.pallas_call(
        _out_body,
        grid=(t // OUT_TM,),
        in_specs=[
            pl.BlockSpec((OUT_TM, d), lambda i: (i, 0)),
            pl.BlockSpec((OUT_TM, w1), lambda i: (i, 0)),
            pl.BlockSpec((OUT_TM, w2), lambda i: (i, 0)),
            pl.BlockSpec((1, w2), lambda i: (0, 0)),
            pl.BlockSpec((w1, d), lambda i: (0, 0)),
            pl.BlockSpec((w2, d), lambda i: (1, 0)),
        ],
        out_specs=pl.BlockSpec((OUT_TM, d), lambda i: (i, 0)),
        out_shape=jax.ShapeDtypeStruct((t, d), F32),
        compiler_params=_params("parallel"),
    )(h, y_rwkv, y_fox, fox_gain, w_out, w_out)


def _norm_body(h_ref, g_ref, o_ref):
    o_ref[...] = _rms_scale(h_ref[...], g_ref[...])


def _final_norm(h, gain):
    t, d = h.shape
    return pl.pallas_call(
        _norm_body,
        grid=(t // OUT_TM,),
        in_specs=[pl.BlockSpec((OUT_TM, d), lambda i: (i, 0)), pl.BlockSpec((1, d), lambda i: (0, 0))],
        out_specs=pl.BlockSpec((OUT_TM, d), lambda i: (i, 0)),
        out_shape=jax.ShapeDtypeStruct((t, d), F32),
        compiler_params=_params("parallel"),
    )(h, gain)


def _pad_rows(w, rows):
    return jnp.pad(w, ((0, rows - w.shape[0]), (0, 0)))


def _pack_w_in(w_in, width, heads):
    d = w_in.shape[0]
    c3 = 3 * width
    c4, c5, c6 = c3 + W_LORA, c3 + W_LORA + A_LORA, c3 + W_LORA + A_LORA + G_LORA
    lora = jnp.zeros((d, LORA_W), w_in.dtype)
    lora = lora.at[:, 0:W_LORA].set(w_in[:, c3:c4])
    lora = lora.at[:, LANES:LANES + A_LORA].set(w_in[:, c4:c5])
    lora = lora.at[:, 2 * LANES:2 * LANES + G_LORA].set(w_in[:, c5:c6])
    lora = lora.at[:, F_LANE:F_LANE + heads].set(w_in[:, c6 + c3:c6 + c3 + heads])
    return jnp.concatenate([w_in[:, :c3], w_in[:, c6:c6 + c3], lora], axis=1).astype(BF16)


def _pack_mu_lora(mu, width):
    c3 = 3 * width
    c4, c5, c6 = c3 + W_LORA, c3 + W_LORA + A_LORA, c3 + W_LORA + A_LORA + G_LORA
    out = jnp.zeros((LORA_W,), mu.dtype)
    out = out.at[0:W_LORA].set(mu[c3:c4])
    out = out.at[LANES:LANES + A_LORA].set(mu[c4:c5])
    out = out.at[2 * LANES:2 * LANES + G_LORA].set(mu[c5:c6])
    return out[None, :]


def kernel(x, meta_tokens, ffn1_norm, ffn1_w_gu, ffn1_w_down, mix_norm, w_in, rwkv_mu, rwkv_w0,
           rwkv_w_up, rwkv_a0, rwkv_a_up, rwkv_g_up, rwkv_k_k, rwkv_k_a, rwkv_r_k, rwkv_lnx_w,
           rwkv_lnx_b, fox_b_f, fox_out_norm, w_out, ffn2_norm, ffn2_w_gu, ffn2_w_down, final_norm):
    b, seq, d = x.shape
    depth = w_in.shape[0]
    width = rwkv_w0.shape[1]
    heads = fox_b_f.shape[1]
    assert rwkv_g_up.shape[1] == G_LORA and G_LORA <= 2 * LANES
    assert F_LANE >= 2 * LANES + G_LORA and F_LANE + heads <= LORA_W and F_LANE % LANES + heads <= LANES
    l = N_META + seq
    lp = -(-l // TIME_ALIGN) * TIME_ALIGN
    t = b * lp
    assert t % PROJ_TM == 0 and t % FFN_TM == 0 and lp % CHUNK == 0

    meta = jnp.broadcast_to(meta_tokens.astype(x.dtype)[None], (b, N_META, d))
    h = jnp.concatenate([meta, x, jnp.zeros((b, lp - l, d), x.dtype)], axis=1).reshape(t, d)

    row = lambda vct: vct.astype(F32)[None, :]
    f_blk = (6 * width + F_LANE) // LANES
    f_lane0 = F_LANE % LANES
    for i in range(depth):
        h = _ffn(h, row(ffn1_norm[i]), ffn1_w_gu[i].astype(BF16), ffn1_w_down[i].astype(BF16))
        p = _in_proj(h, row(mix_norm[i]), _pack_w_in(w_in[i], width, heads))
        p3 = p.reshape(b, lp, p.shape[1])
        prm = dict(
            mu_rkv=row(rwkv_mu[i, :3 * width]), mu_lora=_pack_mu_lora(rwkv_mu[i], width),
            w0=row(rwkv_w0[i]), w_up=_pad_rows(rwkv_w_up[i], LANES).astype(BF16),
            a0=row(rwkv_a0[i]), a_up=_pad_rows(rwkv_a_up[i], LANES).astype(BF16),
            g_up=_pad_rows(rwkv_g_up[i], 2 * LANES).astype(BF16),
            k_k=row(rwkv_k_k[i]), k_a=row(rwkv_k_a[i]), r_k=row(rwkv_r_k[i].reshape(-1)),
            lnx_w=row(rwkv_lnx_w[i]), lnx_b=row(rwkv_lnx_b[i]))
        y_rwkv = _rwkv(p3, prm, width)
        bf_row = jnp.zeros((1, LANES), F32).at[0, f_lane0:f_lane0 + heads].set(fox_b_f[i])
        c_all = _gate_cumsum(p3, bf_row, f_blk)
        c_rows = jnp.transpose(c_all[:, :, f_lane0:f_lane0 + heads], (0, 2, 1))[:, :, None, :]
        y_fox = _attention(p3, c_all, c_rows, width, f_lane0)
        h = _out_proj(h, y_rwkv.reshape(t, width), y_fox.reshape(t, width), row(fox_out_norm[i]),
                      w_out[i].astype(BF16))
        h = _ffn(h, row(ffn2_norm[i]), ffn2_w_gu[i].astype(BF16), ffn2_w_down[i].astype(BF16))
    out = _final_norm(h, row(final_norm))
    return out.reshape(b, lp, d)[:, N_META:l]
```

```python
import functools

import jax
import jax.numpy as jnp
from jax import lax
from jax.experimental import pallas as pl
from jax.experimental.pallas import tpu as pltpu

F32 = jnp.float32
BF16 = jnp.bfloat16

N_META = 16
RWKV_HEAD = 64
FOX_HEAD = 128
W_LORA = 64
A_LORA = 64
G_LORA = 160
NORM_EPS = 1e-6
LNX_EPS = 64e-5
L2_EPS = 1e-12
NEG_INF = -1e30

LANES = 128
VMEM_LIMIT_BYTES = 56 * 1024 * 1024

CHUNK = 64
PAIR = 2 * RWKV_HEAD
TIME_ALIGN = 384
ATT_T = 384
GATE_TB = 384
FFN_TM = 512
FFN_TF = 512
PROJ_TM = 1024
PROJ_TN = 512
OUT_TM = 512
LORA_W = 512
F_LANE = 448

_NT = (((1,), (1,)), ((), ()))


def _rms_scale(x, gain):
    ms = jnp.mean(x * x, axis=-1, keepdims=True)
    return x * lax.rsqrt(ms + NORM_EPS) * gain


def _dot(a, b):
    return jnp.dot(a.astype(BF16), b.astype(BF16), preferred_element_type=F32)


def _dot_nt(a, b):
    return lax.dot_general(a.astype(BF16), b.astype(BF16), _NT, preferred_element_type=F32)


def _split3(x):
    hi = x.astype(BF16)
    r1 = x - hi.astype(F32)
    mid = r1.astype(BF16)
    lo = (r1 - mid.astype(F32)).astype(BF16)
    return hi, mid, lo


def _dot_exact_lhs(m_bf16, x):
    hi, mid, lo = _split3(x)
    acc = jnp.dot(m_bf16, lo, preferred_element_type=F32)
    acc = acc + jnp.dot(m_bf16, mid, preferred_element_type=F32)
    return acc + jnp.dot(m_bf16, hi, preferred_element_type=F32)


def _dot_exact_rhs(x, m_bf16):
    hi, mid, lo = _split3(x)
    acc = jnp.dot(lo, m_bf16, preferred_element_type=F32)
    acc = acc + jnp.dot(mid, m_bf16, preferred_element_type=F32)
    return acc + jnp.dot(hi, m_bf16, preferred_element_type=F32)


def _params(*sem):
    return pltpu.CompilerParams(dimension_semantics=sem, vmem_limit_bytes=VMEM_LIMIT_BYTES)


def _ffn_body(h_ref, g_ref, wg_ref, wu_ref, wd_ref, o_ref, u_ref):
    f = pl.program_id(1)

    @pl.when(f == 0)
    def _():
        x = h_ref[...]
        u_ref[...] = _rms_scale(x, g_ref[...]).astype(BF16)
        o_ref[...] = x

    u = u_ref[...]
    gate = jnp.dot(u, wg_ref[...], preferred_element_type=F32)
    up = jnp.dot(u, wu_ref[...], preferred_element_type=F32)
    act = (0.5 * gate) * jax.nn.sigmoid(gate) * up
    o_ref[...] += jnp.dot(act.astype(BF16), wd_ref[...], preferred_element_type=F32)


def _ffn(h, gain, w_gu, w_down):
    t, d = h.shape
    ff = w_down.shape[0]
    nf = ff // FFN_TF
    return pl.pallas_call(
        _ffn_body,
        grid=(t // FFN_TM, nf),
        in_specs=[
            pl.BlockSpec((FFN_TM, d), lambda i, f: (i, 0)),
            pl.BlockSpec((1, d), lambda i, f: (0, 0)),
            pl.BlockSpec((d, FFN_TF), lambda i, f: (0, f)),
            pl.BlockSpec((d, FFN_TF), lambda i, f: (0, f + nf)),
            pl.BlockSpec((FFN_TF, d), lambda i, f: (f, 0)),
        ],
        out_specs=pl.BlockSpec((FFN_TM, d), lambda i, f: (i, 0)),
        out_shape=jax.ShapeDtypeStruct((t, d), F32),
        scratch_shapes=[pltpu.VMEM((FFN_TM, d), BF16)],
        compiler_params=_params("parallel", "arbitrary"),
    )(h, gain, w_gu, w_gu, w_down)


def _proj_body(h_ref, g_ref, w_ref, o_ref, u_ref, *, scaled_blocks, scale):
    j = pl.program_id(1)

    @pl.when(j == 0)
    def _():
        u_ref[...] = _rms_scale(h_ref[...], g_ref[...]).astype(BF16)

    acc = jnp.dot(u_ref[...], w_ref[...], preferred_element_type=F32)
    if scaled_blocks:
        acc = acc * jnp.where(j < scaled_blocks, scale, 1.0)
    o_ref[...] = acc.astype(o_ref.dtype)


def _in_proj(h, gain, w, out_dtype, scaled_blocks=0, scale=1.0):
    t, d = h.shape
    n = w.shape[1]
    return pl.pallas_call(
        functools.partial(_proj_body, scaled_blocks=scaled_blocks, scale=scale),
        grid=(t // PROJ_TM, n // PROJ_TN),
        in_specs=[
            pl.BlockSpec((PROJ_TM, d), lambda i, j: (i, 0)),
            pl.BlockSpec((1, d), lambda i, j: (0, 0)),
            pl.BlockSpec((d, PROJ_TN), lambda i, j: (0, j)),
        ],
        out_specs=pl.BlockSpec((PROJ_TM, PROJ_TN), lambda i, j: (i, j)),
        out_shape=jax.ShapeDtypeStruct((t, n), out_dtype),
        scratch_shapes=[pltpu.VMEM((PROJ_TM, d), BF16)],
        compiler_params=_params("parallel", "arbitrary"),
    )(h, gain, w)


def _rwkv_body(rkv_ref, lora_ref, mu_rkv_ref, mu_lora_ref, w0_ref, wup_ref, a0_ref, aup_ref,
               gup_ref, kk_ref, ka_ref, rk_ref, lnw_ref, lnb_ref, o_ref,
               carry_rkv, carry_lora, st_ref, *, width):
    c = pl.program_id(1)

    @pl.when(c == 0)
    def _():
        carry_rkv[...] = jnp.zeros_like(carry_rkv)
        carry_lora[...] = jnp.zeros_like(carry_lora)
        st_ref[...] = jnp.zeros_like(st_ref)

    row = lax.broadcasted_iota(jnp.int32, (CHUNK, 1), 0)

    def shift(p, carry_ref, mu):
        prev = jnp.where(row == 0, carry_ref[...], pltpu.roll(p, 1, 0))
        carry_ref[...] = p[CHUNK - 1:CHUNK, :]
        return p + (prev - p) * mu

    pf = shift(rkv_ref[0], carry_rkv, mu_rkv_ref[...])
    lf = shift(lora_ref[0], carry_lora, mu_lora_ref[...])
    r = pf[:, :width]
    k = pf[:, width:2 * width]
    v = pf[:, 2 * width:]
    wd = lf[:, 0:LANES]
    ad = lf[:, LANES:2 * LANES]
    gd = lf[:, 2 * LANES:4 * LANES]

    z = w0_ref[...] + _dot(jnp.tanh(wd), wup_ref[...])
    nz = -z
    softplus = jnp.maximum(nz, 0.0) + jnp.log1p(jnp.exp(-jnp.abs(nz)))
    logw = -jnp.exp(-softplus - 0.5)
    a = jax.nn.sigmoid(a0_ref[...] + _dot(ad, aup_ref[...]))
    g = _dot(jax.nn.sigmoid(gd), gup_ref[...])

    ci = lax.broadcasted_iota(jnp.int32, (CHUNK, CHUNK), 0)
    cj = lax.broadcasted_iota(jnp.int32, (CHUNK, CHUNK), 1)
    ltri = (cj <= ci).astype(BF16)
    cs = _dot_exact_lhs(ltri, logw)

    ii = lax.broadcasted_iota(jnp.int32, (PAIR, PAIR), 0)
    jj = lax.broadcasted_iota(jnp.int32, (PAIR, PAIR), 1)
    same_head = (ii // RWKV_HEAD) == (jj // RWKV_HEAD)
    mask_strict = same_head & (jj < ii)
    mask_incl = same_head & (jj <= ii)
    eye = ii == jj
    seg_ones = same_head.astype(BF16)
    lane = lax.broadcasted_iota(jnp.int32, (CHUNK, PAIR), 1)
    head0 = lane < RWKV_HEAD

    def stack(x):
        return jnp.concatenate([jnp.where(head0, x, 0.0), jnp.where(head0, 0.0, x)], axis=0)

    def both(x):
        return jnp.concatenate([x, x], axis=0)

    pairs = range(width // PAIR)
    sls = [slice(hp * PAIR, (hp + 1) * PAIR) for hp in pairs]

    kk = [k[:, sl] * kk_ref[:, sl] for sl in sls]
    ss = [_dot_exact_rhs(x * x, seg_ones) for x in kk]
    kk = [x / jnp.maximum(jnp.sqrt(s), L2_EPS) for x, s in zip(kk, ss)]
    k2 = [k[:, sl] * (1.0 + (a[:, sl] - 1.0) * ka_ref[:, sl]) for sl in sls]
    bv = [x * a[:, sl] for x, sl in zip(kk, sls)]
    last = [cs[CHUNK - 1:CHUNK, sl] for sl in sls]
    e_neg = [jnp.exp(-cs[:, sl]) for sl in sls]
    e_end = [jnp.exp(lst - cs[:, sl]) for lst, sl in zip(last, sls)]
    xa = [stack(-x * jnp.exp(cs[:, sl] - logw[:, sl])) for x, sl in zip(kk, sls)]
    xr = [stack(r[:, sl] * jnp.exp(cs[:, sl])) for sl in sls]
    xbh_t = [stack(x * e).T for x, e in zip(bv, e_end)]
    xkh_t = [stack(x * e).T for x, e in zip(k2, e_end)]
    vst = [stack(v[:, sl]) for sl in sls]

    big = [_dot_nt(jnp.concatenate([xa[p], xr[p]], axis=0),
                   jnp.concatenate([both(bv[p] * e_neg[p]), both(k2[p] * e_neg[p])], axis=0))
           for p in pairs]
    a_ab = [jnp.where(mask_strict, x[:PAIR, :PAIR], 0.0) for x in big]
    a_ak = [jnp.where(mask_strict, x[:PAIR, PAIR:], 0.0) for x in big]
    a_rb = [jnp.where(mask_incl, x[PAIR:, :PAIR], 0.0) for x in big]
    a_rk = [jnp.where(mask_incl, x[PAIR:, PAIR:], 0.0) for x in big]

    lvl1 = eye | (((ii ^ jj) == 1) & (jj < ii))
    tinv = [jnp.where(lvl1, jnp.where(eye, 1.0, x), 0.0) for x in a_ab]
    s = 2
    while s < RWKV_HEAD:
        e_mask = ((ii // (2 * s)) == (jj // (2 * s))) & (((ii // s) % 2) == 1) & (((jj // s) % 2) == 0)
        half = [_dot(t_, jnp.where(e_mask, x, 0.0)) for t_, x in zip(tinv, a_ab)]
        tinv = [t_ + _dot(h_, t_) for t_, h_ in zip(tinv, half)]
        s *= 2

    w_av = [_dot(a_ak[p], vst[p]) for p in pairs]
    x = [_dot(tinv[p], jnp.concatenate([xa[p], w_av[p]], axis=1)) for p in pairs]
    mg = [_dot(xbh_t[p], x[p]) for p in pairs]
    kv = [_dot(xkh_t[p], vst[p]) for p in pairs]
    ry = [_dot(a_rb[p], x[p]) for p in pairs]
    rkv_ = [_dot(a_rk[p], vst[p]) for p in pairs]
    st = [st_ref[p] for p in pairs]
    yst = [_dot(xr[p] + ry[p][:, :PAIR], st[p]) + ry[p][:, PAIR:] + rkv_[p] for p in pairs]
    for p in pairs:
        m = mg[p][:, :PAIR] + jnp.where(eye, jnp.exp(last[p]), 0.0)
        st_ref[p] = _dot(m, st[p]) + mg[p][:, PAIR:] + kv[p]
    y = [x_[:CHUNK] + x_[CHUNK:] for x_ in yst]

    inv_n = 1.0 / RWKV_HEAD
    mean = [_dot_exact_rhs(x_, seg_ones) * inv_n for x_ in y]
    dlt = [x_ - m_ for x_, m_ in zip(y, mean)]
    var = [_dot_exact_rhs(x_ * x_, seg_ones) * inv_n for x_ in dlt]
    bonus = [_dot_exact_rhs(r[:, sl] * k2[p] * rk_ref[:, sl], seg_ones) * v[:, sl]
             for p, sl in zip(pairs, sls)]
    for p, sl in zip(pairs, sls):
        yn = dlt[p] * lax.rsqrt(var[p] + LNX_EPS) * lnw_ref[:, sl] + lnb_ref[:, sl]
        o_ref[0, :, sl] = ((yn + bonus[p]) * g[:, sl]).astype(o_ref.dtype)


def _rwkv(p3, prm, width):
    b, lp, _ = p3.shape
    n_rkv = 3 * width
    vec = lambda n: pl.BlockSpec((1, n), lambda i, c: (0, 0))
    mat = lambda k: pl.BlockSpec((k, width), lambda i, c: (0, 0))
    return pl.pallas_call(
        functools.partial(_rwkv_body, width=width),
        grid=(b, lp // CHUNK),
        in_specs=[
            pl.BlockSpec((1, CHUNK, n_rkv), lambda i, c: (i, c, 0)),
            pl.BlockSpec((1, CHUNK, LORA_W), lambda i, c: (i, c, n_rkv // LORA_W)),
            vec(n_rkv), vec(LORA_W), vec(width), mat(LANES), vec(width), mat(LANES),
            mat(2 * LANES), vec(width), vec(width), vec(width), vec(width), vec(width),
        ],
        out_specs=pl.BlockSpec((1, CHUNK, width), lambda i, c: (i, c, 0)),
        out_shape=jax.ShapeDtypeStruct((b, lp, width), BF16),
        scratch_shapes=[
            pltpu.VMEM((1, n_rkv), F32),
            pltpu.VMEM((1, LORA_W), F32),
            pltpu.VMEM((width // PAIR, PAIR, PAIR), F32),
        ],
        compiler_params=_params("parallel", "arbitrary"),
    )(p3, p3, prm["mu_rkv"], prm["mu_lora"], prm["w0"], prm["w_up"], prm["a0"], prm["a_up"],
      prm["g_up"], prm["k_k"], prm["k_a"], prm["r_k"], prm["lnx_w"], prm["lnx_b"])


def _gate_body(x_ref, bf_ref, o_ref, carry_ref):
    @pl.when(pl.program_id(1) == 0)
    def _():
        carry_ref[...] = jnp.zeros_like(carry_ref)

    zz = x_ref[0] + bf_ref[...]
    log_f = jnp.minimum(zz, 0.0) - jnp.log1p(jnp.exp(-jnp.abs(zz)))
    ci = lax.broadcasted_iota(jnp.int32, (GATE_TB, GATE_TB), 0)
    cj = lax.broadcasted_iota(jnp.int32, (GATE_TB, GATE_TB), 1)
    csum = _dot_exact_lhs((cj <= ci).astype(BF16), log_f) + carry_ref[...]
    carry_ref[...] = csum[GATE_TB - 1:GATE_TB, :]
    o_ref[0] = csum


def _gate_cumsum(p3, bf_row, lane_blk):
    b, lp, _ = p3.shape
    return pl.pallas_call(
        _gate_body,
        grid=(b, lp // GATE_TB),
        in_specs=[
            pl.BlockSpec((1, GATE_TB, LANES), lambda i, c: (i, c, lane_blk)),
            pl.BlockSpec((1, LANES), lambda i, c: (0, 0)),
        ],
        out_specs=pl.BlockSpec((1, GATE_TB, LANES), lambda i, c: (i, c, 0)),
        out_shape=jax.ShapeDtypeStruct((b, lp, LANES), F32),
        scratch_shapes=[pltpu.VMEM((1, LANES), F32)],
        compiler_params=_params("parallel", "arbitrary"),
    )(p3, bf_row)


def _attn_body(q_ref, k_ref, v_ref, ck_ref, o_ref, m_ref, l_ref, acc_ref):
    qi = pl.program_id(2)
    m_ref[...] = jnp.full_like(m_ref, NEG_INF)
    l_ref[...] = jnp.zeros_like(l_ref)
    acc_ref[...] = jnp.zeros_like(acc_ref)
    q = q_ref[0]

    def block(j, masked):
        off = pl.multiple_of(j * ATT_T, ATT_T)
        ks = k_ref[0, pl.ds(off, ATT_T), :]
        vs = v_ref[0, pl.ds(off, ATT_T), :]
        s = lax.dot_general(q, ks, _NT, preferred_element_type=F32) - ck_ref[0, 0, :, pl.ds(off, ATT_T)]
        if masked:
            rr = lax.broadcasted_iota(jnp.int32, (ATT_T, ATT_T), 0)
            cc = lax.broadcasted_iota(jnp.int32, (ATT_T, ATT_T), 1)
            s = jnp.where(cc <= rr, s, NEG_INF)
        m_prev = m_ref[...]
        m_new = jnp.maximum(m_prev, jnp.max(s, axis=-1, keepdims=True))
        alpha = jnp.exp(m_prev - m_new)
        p = jnp.exp(s - m_new)
        l_ref[...] = alpha * l_ref[...] + jnp.sum(p, axis=-1, keepdims=True)
        acc_ref[...] = alpha * acc_ref[...] + jnp.dot(p.astype(BF16), vs, preferred_element_type=F32)
        m_ref[...] = m_new

    def body(j, carry):
        block(j, False)
        return carry

    lax.fori_loop(0, qi, body, 0)
    block(qi, True)
    o_ref[0] = acc_ref[...] / l_ref[...]


def _attention(qkv3, c_rows, width):
    b, lp, _ = qkv3.shape
    heads = width // FOX_HEAD
    return pl.pallas_call(
        _attn_body,
        grid=(b, heads, lp // ATT_T),
        in_specs=[
            pl.BlockSpec((1, ATT_T, FOX_HEAD), lambda i, h, q: (i, q, h)),
            pl.BlockSpec((1, lp, FOX_HEAD), lambda i, h, q: (i, 0, heads + h)),
            pl.BlockSpec((1, lp, FOX_HEAD), lambda i, h, q: (i, 0, 2 * heads + h)),
            pl.BlockSpec((1, 1, 1, lp), lambda i, h, q: (i, h, 0, 0)),
        ],
        out_specs=pl.BlockSpec((1, ATT_T, FOX_HEAD), lambda i, h, q: (i, q, h)),
        out_shape=jax.ShapeDtypeStruct((b, lp, width), F32),
        scratch_shapes=[
            pltpu.VMEM((ATT_T, 1), F32),
            pltpu.VMEM((ATT_T, 1), F32),
            pltpu.VMEM((ATT_T, FOX_HEAD), F32),
        ],
        compiler_params=_params("parallel", "parallel", "arbitrary"),
    )(qkv3, qkv3, qkv3, c_rows)


def _out_body(h_ref, yr_ref, yf_ref, gn_ref, w1_ref, w2_ref, o_ref):
    yf = _rms_scale(yf_ref[...], gn_ref[...])
    acc = jnp.dot(yr_ref[...], w1_ref[...], preferred_element_type=F32)
    acc = acc + jnp.dot(yf.astype(BF16), w2_ref[...], preferred_element_type=F32)
    o_ref[...] = h_ref[...] + acc


def _out_proj(h, y_rwkv, y_fox, fox_gain, w_out):
    t, d = h.shape
    w1 = y_rwkv.shape[1]
    w2 = y_fox.shape[1]
    return pl.pallas_call(
        _out_body,
        grid=(t // OUT_TM,),
        in_specs=[
            pl.BlockSpec((OUT_TM, d), lambda i: (i, 0)),
            pl.BlockSpec((OUT_TM, w1), lambda i: (i, 0)),
            pl.BlockSpec((OUT_TM, w2), lambda i: (i, 0)),
            pl.BlockSpec((1, w2), lambda i: (0, 0)),
            pl.BlockSpec((w1, d), lambda i: (0, 0)),
            pl.BlockSpec((w2, d), lambda i: (1, 0)),
        ],
        out_specs=pl.BlockSpec((OUT_TM, d), lambda i: (i, 0)),
        out_shape=jax.ShapeDtypeStruct((t, d), F32),
        compiler_params=_params("parallel"),
    )(h, y_rwkv, y_fox, fox_gain, w_out, w_out)


def _norm_body(h_ref, g_ref, o_ref):
    o_ref[...] = _rms_scale(h_ref[...], g_ref[...])


def _final_norm(h, gain):
    t, d = h.shape
    return pl.pallas_call(
        _norm_body,
        grid=(t // OUT_TM,),
        in_specs=[pl.BlockSpec((OUT_TM, d), lambda i: (i, 0)), pl.BlockSpec((1, d), lambda i: (0, 0))],
        out_specs=pl.BlockSpec((OUT_TM, d), lambda i: (i, 0)),
        out_shape=jax.ShapeDtypeStruct((t, d), F32),
        compiler_params=_params("parallel"),
    )(h, gain)


def _pad_rows(w, rows):
    return jnp.pad(w, ((0, rows - w.shape[0]), (0, 0)))


def _pack_w_rwkv(w_in, width, heads):
    d = w_in.shape[0]
    c3 = 3 * width
    c4, c5, c6 = c3 + W_LORA, c3 + W_LORA + A_LORA, c3 + W_LORA + A_LORA + G_LORA
    lora = jnp.zeros((d, LORA_W), w_in.dtype)
    lora = lora.at[:, 0:W_LORA].set(w_in[:, c3:c4])
    lora = lora.at[:, LANES:LANES + A_LORA].set(w_in[:, c4:c5])
    lora = lora.at[:, 2 * LANES:2 * LANES + G_LORA].set(w_in[:, c5:c6])
    lora = lora.at[:, F_LANE:F_LANE + heads].set(w_in[:, c6 + c3:c6 + c3 + heads])
    return jnp.concatenate([w_in[:, :c3], lora], axis=1).astype(BF16)


def _pack_mu_lora(mu, width):
    c3 = 3 * width
    c4, c5, c6 = c3 + W_LORA, c3 + W_LORA + A_LORA, c3 + W_LORA + A_LORA + G_LORA
    out = jnp.zeros((LORA_W,), mu.dtype)
    out = out.at[0:W_LORA].set(mu[c3:c4])
    out = out.at[LANES:LANES + A_LORA].set(mu[c4:c5])
    out = out.at[2 * LANES:2 * LANES + G_LORA].set(mu[c5:c6])
    return out[None, :]


def kernel(x, meta_tokens, ffn1_norm, ffn1_w_gu, ffn1_w_down, mix_norm, w_in, rwkv_mu, rwkv_w0,
           rwkv_w_up, rwkv_a0, rwkv_a_up, rwkv_g_up, rwkv_k_k, rwkv_k_a, rwkv_r_k, rwkv_lnx_w,
           rwkv_lnx_b, fox_b_f, fox_out_norm, w_out, ffn2_norm, ffn2_w_gu, ffn2_w_down, final_norm):
    b, seq, d = x.shape
    depth = w_in.shape[0]
    width = rwkv_w0.shape[1]
    heads = fox_b_f.shape[1]
    assert rwkv_g_up.shape[1] == G_LORA and G_LORA <= 2 * LANES
    assert F_LANE >= 2 * LANES + G_LORA and F_LANE + heads <= LORA_W and F_LANE % LANES + heads <= LANES
    l = N_META + seq
    lp = -(-l // TIME_ALIGN) * TIME_ALIGN
    t = b * lp
    assert t % PROJ_TM == 0 and t % FFN_TM == 0 and lp % CHUNK == 0
    assert width % PROJ_TN == 0

    meta = jnp.broadcast_to(meta_tokens.astype(x.dtype)[None], (b, N_META, d))
    h = jnp.concatenate([meta, x, jnp.zeros((b, lp - l, d), x.dtype)], axis=1).reshape(t, d)

    row = lambda vct: vct.astype(F32)[None, :]
    c6 = 3 * width + W_LORA + A_LORA + G_LORA
    f_blk = (3 * width + F_LANE) // LANES
    f_lane0 = F_LANE % LANES
    for i in range(depth):
        h = _ffn(h, row(ffn1_norm[i]), ffn1_w_gu[i].astype(BF16), ffn1_w_down[i].astype(BF16))
        gain = row(mix_norm[i])
        p_rwkv = _in_proj(h, gain, _pack_w_rwkv(w_in[i], width, heads), F32)
        qkv = _in_proj(h, gain, w_in[i, :, c6:c6 + 3 * width].astype(BF16), BF16,
                       scaled_blocks=width // PROJ_TN, scale=FOX_HEAD ** -0.5)
        p3 = p_rwkv.reshape(b, lp, p_rwkv.shape[1])
        prm = dict(
            mu_rkv=row(rwkv_mu[i, :3 * width]), mu_lora=_pack_mu_lora(rwkv_mu[i], width),
            w0=row(rwkv_w0[i]), w_up=_pad_rows(rwkv_w_up[i], LANES).astype(BF16),
            a0=row(rwkv_a0[i]), a_up=_pad_rows(rwkv_a_up[i], LANES).astype(BF16),
            g_up=_pad_rows(rwkv_g_up[i], 2 * LANES).astype(BF16),
            k_k=row(rwkv_k_k[i]), k_a=row(rwkv_k_a[i]), r_k=row(rwkv_r_k[i].reshape(-1)),
            lnx_w=row(rwkv_lnx_w[i]), lnx_b=row(rwkv_lnx_b[i]))
        y_rwkv = _rwkv(p3, prm, width)
        bf_row = jnp.zeros((1, LANES), F32).at[0, f_lane0:f_lane0 + heads].set(fox_b_f[i])
        c_all = _gate_cumsum(p3, bf_row, f_blk)
        c_rows = jnp.transpose(c_all[:, :, f_lane0:f_lane0 + heads], (0, 2, 1))[:, :, None, :]
        y_fox = _attention(qkv.reshape(b, lp, 3 * width), c_rows, width)
        h = _out_proj(h, y_rwkv.reshape(t, width), y_fox.reshape(t, width), row(fox_out_norm[i]),
                      w_out[i].astype(BF16))
        h = _ffn(h, row(ffn2_norm[i]), ffn2_w_gu[i].astype(BF16), ffn2_w_down[i].astype(BF16))
    out = _final_norm(h, row(final_norm))
    return out.reshape(b, lp, d)[:, N_META:l]
```

```python
import functools
import math

import jax
import jax.numpy as jnp
from jax import lax
from jax.experimental import pallas as pl
from jax.experimental.pallas import tpu as pltpu

F32 = jnp.float32
BF16 = jnp.bfloat16

N_META = 16
RWKV_HEAD = 64
FOX_HEAD = 128
W_LORA = 64
A_LORA = 64
G_LORA = 160
NORM_EPS = 1e-6
LNX_EPS = 64e-5
L2_EPS = 1e-12
NEG_INF = -1e30
LOG2E = math.log2(math.e)

LANES = 128
VMEM_LIMIT_BYTES = 56 * 1024 * 1024

CHUNK = 64
RWKV_ROWS = 128
PAIR = 2 * RWKV_HEAD
TIME_ALIGN = 384
ATT_T = 384
GATE_TB = 384
FFN_TM = 512
FFN_TF = 512
PROJ_TM = 1024
PROJ_TN = 512
OUT_TM = 512
LORA_W = 512
F_LANE = 448

_NT = (((1,), (1,)), ((), ()))


def _rms_scale(x, gain):
    ms = jnp.mean(x * x, axis=-1, keepdims=True)
    return x * lax.rsqrt(ms + NORM_EPS) * gain


def _dot(a, b):
    return jnp.dot(a.astype(BF16), b.astype(BF16), preferred_element_type=F32)


def _dot_nt(a, b):
    return lax.dot_general(a.astype(BF16), b.astype(BF16), _NT, preferred_element_type=F32)


def _split3(x):
    hi = x.astype(BF16)
    r1 = x - hi.astype(F32)
    mid = r1.astype(BF16)
    lo = (r1 - mid.astype(F32)).astype(BF16)
    return hi, mid, lo


def _dot_exact_lhs(m_bf16, x):
    hi, mid, lo = _split3(x)
    acc = jnp.dot(m_bf16, lo, preferred_element_type=F32)
    acc = acc + jnp.dot(m_bf16, mid, preferred_element_type=F32)
    return acc + jnp.dot(m_bf16, hi, preferred_element_type=F32)


def _params(*sem):
    return pltpu.CompilerParams(dimension_semantics=sem, vmem_limit_bytes=VMEM_LIMIT_BYTES)


def _ffn_body(h_ref, g_ref, wg_ref, wu_ref, wd_ref, o_ref, u_ref):
    f = pl.program_id(1)

    @pl.when(f == 0)
    def _():
        x = h_ref[...]
        u_ref[...] = _rms_scale(x, g_ref[...]).astype(BF16)
        o_ref[...] = x

    u = u_ref[...]
    gate = jnp.dot(u, wg_ref[...], preferred_element_type=F32)
    up = jnp.dot(u, wu_ref[...], preferred_element_type=F32)
    act = (0.5 * gate) * jax.nn.sigmoid(gate) * up
    o_ref[...] += jnp.dot(act.astype(BF16), wd_ref[...], preferred_element_type=F32)


def _ffn(h, gain, w_gu, w_down):
    t, d = h.shape
    ff = w_down.shape[0]
    nf = ff // FFN_TF
    return pl.pallas_call(
        _ffn_body,
        grid=(t // FFN_TM, nf),
        in_specs=[
            pl.BlockSpec((FFN_TM, d), lambda i, f: (i, 0)),
            pl.BlockSpec((1, d), lambda i, f: (0, 0)),
            pl.BlockSpec((d, FFN_TF), lambda i, f: (0, f)),
            pl.BlockSpec((d, FFN_TF), lambda i, f: (0, f + nf)),
            pl.BlockSpec((FFN_TF, d), lambda i, f: (f, 0)),
        ],
        out_specs=pl.BlockSpec((FFN_TM, d), lambda i, f: (i, 0)),
        out_shape=jax.ShapeDtypeStruct((t, d), F32),
        scratch_shapes=[pltpu.VMEM((FFN_TM, d), BF16)],
        compiler_params=_params("parallel", "arbitrary"),
    )(h, gain, w_gu, w_gu, w_down)


def _proj_body(h_ref, g_ref, w_ref, o_ref, u_ref, *, scaled_blocks, scale):
    j = pl.program_id(1)

    @pl.when(j == 0)
    def _():
        u_ref[...] = _rms_scale(h_ref[...], g_ref[...]).astype(BF16)

    acc = jnp.dot(u_ref[...], w_ref[...], preferred_element_type=F32)
    if scaled_blocks:
        acc = acc * jnp.where(j < scaled_blocks, scale, 1.0)
    o_ref[...] = acc.astype(o_ref.dtype)


def _in_proj(h, gain, w, out_dtype, scaled_blocks=0, scale=1.0):
    t, d = h.shape
    n = w.shape[1]
    return pl.pallas_call(
        functools.partial(_proj_body, scaled_blocks=scaled_blocks, scale=scale),
        grid=(t // PROJ_TM, n // PROJ_TN),
        in_specs=[
            pl.BlockSpec((PROJ_TM, d), lambda i, j: (i, 0)),
            pl.BlockSpec((1, d), lambda i, j: (0, 0)),
            pl.BlockSpec((d, PROJ_TN), lambda i, j: (0, j)),
        ],
        out_specs=pl.BlockSpec((PROJ_TM, PROJ_TN), lambda i, j: (i, j)),
        out_shape=jax.ShapeDtypeStruct((t, n), out_dtype),
        scratch_shapes=[pltpu.VMEM((PROJ_TM, d), BF16)],
        compiler_params=_params("parallel", "arbitrary"),
    )(h, gain, w)


def _rwkv_body(rkv_ref, lora_ref, mu_rkv_ref, mu_lora_ref, w0_ref, wup_ref, a0_ref, aup_ref,
               gup_ref, kk_ref, ka_ref, rk_ref, lnw_ref, lnb_ref, o_ref,
               carry_rkv, carry_lora, st_ref, *, width):
    rows = RWKV_ROWS
    n_ch = rows // CHUNK
    c = pl.program_id(1)

    @pl.when(c == 0)
    def _():
        carry_rkv[...] = jnp.zeros_like(carry_rkv)
        carry_lora[...] = jnp.zeros_like(carry_lora)
        st_ref[...] = jnp.zeros_like(st_ref)

    row = lax.broadcasted_iota(jnp.int32, (rows, 1), 0)

    def shift(p, carry_ref, mu):
        prev = jnp.where(row == 0, carry_ref[...], pltpu.roll(p, 1, 0))
        carry_ref[...] = p[rows - 1:rows, :]
        return p + (prev - p) * mu

    pf = shift(rkv_ref[0], carry_rkv, mu_rkv_ref[...])
    lf = shift(lora_ref[0], carry_lora, mu_lora_ref[...])
    r = pf[:, :width]
    k = pf[:, width:2 * width]
    v = pf[:, 2 * width:]
    wd = lf[:, 0:LANES]
    ad = lf[:, LANES:2 * LANES]
    gd = lf[:, 2 * LANES:4 * LANES]

    z = w0_ref[...] + _dot(jnp.tanh(wd), wup_ref[...])
    nz = -z
    softplus = jnp.maximum(nz, 0.0) + jnp.log1p(jnp.exp(-jnp.abs(nz)))
    logw = -jnp.exp(-softplus - 0.5)
    a = jax.nn.sigmoid(a0_ref[...] + _dot(ad, aup_ref[...]))
    g = _dot(jax.nn.sigmoid(gd), gup_ref[...])

    ci = lax.broadcasted_iota(jnp.int32, (rows, rows), 0)
    cj = lax.broadcasted_iota(jnp.int32, (rows, rows), 1)
    ltri = ((cj <= ci) & ((ci // CHUNK) == (cj // CHUNK))).astype(BF16)
    cs = _dot_exact_lhs(ltri, logw)

    ii = lax.broadcasted_iota(jnp.int32, (PAIR, PAIR), 0)
    jj = lax.broadcasted_iota(jnp.int32, (PAIR, PAIR), 1)
    same_head = (ii // RWKV_HEAD) == (jj // RWKV_HEAD)
    mask_strict = same_head & (jj < ii)
    mask_incl = same_head & (jj <= ii)
    eye = ii == jj
    seg_ones = same_head.astype(BF16)
    lane = lax.broadcasted_iota(jnp.int32, (CHUNK, PAIR), 1)
    head0 = lane < RWKV_HEAD

    def stack(x):
        return jnp.concatenate([jnp.where(head0, x, 0.0), jnp.where(head0, 0.0, x)], axis=0)

    def both(x):
        return jnp.concatenate([x, x], axis=0)

    pairs = range(width // PAIR)
    sls = [slice(hp * PAIR, (hp + 1) * PAIR) for hp in pairs]
    chunks = [slice(ch * CHUNK, (ch + 1) * CHUNK) for ch in range(n_ch)]
    units = [(ch, p) for ch in range(n_ch) for p in pairs]

    kk = [k[:, sl] * kk_ref[:, sl] for sl in sls]
    ss = [_dot(x * x, seg_ones) for x in kk]
    kk = [x / jnp.maximum(jnp.sqrt(s), L2_EPS) for x, s in zip(kk, ss)]
    k2 = [k[:, sl] * (1.0 + (a[:, sl] - 1.0) * ka_ref[:, sl]) for sl in sls]
    bv = [x * a[:, sl] for x, sl in zip(kk, sls)]
    e_neg = [jnp.exp(-cs[:, sl]) for sl in sls]
    at_ = [-kk[p] * jnp.exp(cs[:, sls[p]] - logw[:, sls[p]]) for p in pairs]
    rt_ = [r[:, sl] * jnp.exp(cs[:, sl]) for sl in sls]
    bt_ = [bv[p] * e_neg[p] for p in pairs]
    kt_ = [k2[p] * e_neg[p] for p in pairs]

    last = [cs[chunks[ch]][CHUNK - 1:CHUNK, sls[p]] for ch, p in units]
    e_end = [jnp.exp(last[u] - cs[chunks[ch], sls[p]]) for u, (ch, p) in enumerate(units)]
    xa = [stack(at_[p][chunks[ch]]) for ch, p in units]
    xr = [stack(rt_[p][chunks[ch]]) for ch, p in units]
    xbh_t = [stack(bv[p][chunks[ch]] * e_end[u]).T for u, (ch, p) in enumerate(units)]
    xkh_t = [stack(k2[p][chunks[ch]] * e_end[u]).T for u, (ch, p) in enumerate(units)]
    vst = [stack(v[chunks[ch], sls[p]]) for ch, p in units]

    big = [_dot_nt(jnp.concatenate([xa[u], xr[u]], axis=0),
                   jnp.concatenate([both(bt_[p][chunks[ch]]), both(kt_[p][chunks[ch]])], axis=0))
           for u, (ch, p) in enumerate(units)]
    a_ab = [jnp.where(mask_strict, x[:PAIR, :PAIR], 0.0) for x in big]
    a_ak = [jnp.where(mask_strict, x[:PAIR, PAIR:], 0.0) for x in big]
    a_rb = [jnp.where(mask_incl, x[PAIR:, :PAIR], 0.0) for x in big]
    a_rk = [jnp.where(mask_incl, x[PAIR:, PAIR:], 0.0) for x in big]

    lvl1 = eye | (((ii ^ jj) == 1) & (jj < ii))
    tinv = [jnp.where(lvl1, jnp.where(eye, 1.0, x), 0.0) for x in a_ab]
    s = 2
    while s < RWKV_HEAD:
        e_mask = ((ii // (2 * s)) == (jj // (2 * s))) & (((ii // s) % 2) == 1) & (((jj // s) % 2) == 0)
        half = [_dot(t_, jnp.where(e_mask, x, 0.0)) for t_, x in zip(tinv, a_ab)]
        tinv = [t_ + _dot(h_, t_) for t_, h_ in zip(tinv, half)]
        s *= 2

    n_u = range(len(units))
    w_av = [_dot(a_ak[u], vst[u]) for u in n_u]
    x = [_dot(tinv[u], jnp.concatenate([xa[u], w_av[u]], axis=1)) for u in n_u]
    mg = [_dot(xbh_t[u], x[u]) for u in n_u]
    kv = [_dot(xkh_t[u], vst[u]) for u in n_u]
    ry = [_dot(a_rb[u], x[u]) for u in n_u]
    rkv_ = [_dot(a_rk[u], vst[u]) for u in n_u]
    rh = [xr[u] + ry[u][:, :PAIR] for u in n_u]
    y0 = [ry[u][:, PAIR:] + rkv_[u] for u in n_u]
    mm = [mg[u][:, :PAIR] + jnp.where(eye, jnp.exp(last[u]), 0.0) for u in n_u]
    gg = [mg[u][:, PAIR:] + kv[u] for u in n_u]

    st = [st_ref[p] for p in pairs]
    yst = []
    for ch in range(n_ch):
        base = ch * len(pairs)
        yst.append([_dot(rh[base + p], st[p]) + y0[base + p] for p in pairs])
        st = [_dot(mm[base + p], st[p]) + gg[base + p] for p in pairs]
    for p in pairs:
        st_ref[p] = st[p]
    y = [jnp.concatenate([yst[ch][p][:CHUNK] + yst[ch][p][CHUNK:] for ch in range(n_ch)], axis=0)
         for p in pairs]

    inv_n = 1.0 / RWKV_HEAD
    mean = [_dot(x_, seg_ones) * inv_n for x_ in y]
    dlt = [x_ - m_ for x_, m_ in zip(y, mean)]
    var = [_dot(x_ * x_, seg_ones) * inv_n for x_ in dlt]
    bonus = [_dot(r[:, sl] * k2[p] * rk_ref[:, sl], seg_ones) * v[:, sl] for p, sl in zip(pairs, sls)]
    for p, sl in zip(pairs, sls):
        yn = dlt[p] * lax.rsqrt(var[p] + LNX_EPS) * lnw_ref[:, sl] + lnb_ref[:, sl]
        o_ref[0, :, sl] = ((yn + bonus[p]) * g[:, sl]).astype(o_ref.dtype)


def _rwkv(p3, prm, width):
    b, lp, _ = p3.shape
    n_rkv = 3 * width
    vec = lambda n: pl.BlockSpec((1, n), lambda i, c: (0, 0))
    mat = lambda k: pl.BlockSpec((k, width), lambda i, c: (0, 0))
    return pl.pallas_call(
        functools.partial(_rwkv_body, width=width),
        grid=(b, lp // RWKV_ROWS),
        in_specs=[
            pl.BlockSpec((1, RWKV_ROWS, n_rkv), lambda i, c: (i, c, 0)),
            pl.BlockSpec((1, RWKV_ROWS, LORA_W), lambda i, c: (i, c, n_rkv // LORA_W)),
            vec(n_rkv), vec(LORA_W), vec(width), mat(LANES), vec(width), mat(LANES),
            mat(2 * LANES), vec(width), vec(width), vec(width), vec(width), vec(width),
        ],
        out_specs=pl.BlockSpec((1, RWKV_ROWS, width), lambda i, c: (i, c, 0)),
        out_shape=jax.ShapeDtypeStruct((b, lp, width), BF16),
        scratch_shapes=[
            pltpu.VMEM((1, n_rkv), F32),
            pltpu.VMEM((1, LORA_W), F32),
            pltpu.VMEM((width // PAIR, PAIR, PAIR), F32),
        ],
        compiler_params=_params("parallel", "arbitrary"),
    )(p3, p3, prm["mu_rkv"], prm["mu_lora"], prm["w0"], prm["w_up"], prm["a0"], prm["a_up"],
      prm["g_up"], prm["k_k"], prm["k_a"], prm["r_k"], prm["lnx_w"], prm["lnx_b"])


def _gate_body(x_ref, bf_ref, o_ref, carry_ref, *, heads, lane0):
    @pl.when(pl.program_id(1) == 0)
    def _():
        carry_ref[...] = jnp.zeros_like(carry_ref)

    zz = x_ref[0] + bf_ref[...]
    log_f = jnp.minimum(zz, 0.0) - jnp.log1p(jnp.exp(-jnp.abs(zz)))
    ci = lax.broadcasted_iota(jnp.int32, (GATE_TB, GATE_TB), 0)
    cj = lax.broadcasted_iota(jnp.int32, (GATE_TB, GATE_TB), 1)
    csum = _dot_exact_lhs((cj <= ci).astype(BF16), log_f) + carry_ref[...]
    carry_ref[...] = csum[GATE_TB - 1:GATE_TB, :]
    pieces = _split3(csum * LOG2E)
    src = lax.broadcasted_iota(jnp.int32, (LANES, LANES), 0)
    dst = lax.broadcasted_iota(jnp.int32, (LANES, LANES), 1)
    for g in range(heads):
        acc = jnp.zeros((GATE_TB, LANES), F32)
        for i, piece in enumerate(pieces):
            move = ((src == lane0 + g) & (dst == i)).astype(BF16)
            acc = acc + jnp.dot(piece, move, preferred_element_type=F32)
        o_ref[0, g] = acc.astype(BF16)


def _gate_cumsum(p3, bf_row, lane_blk, heads, lane0):
    b, lp, _ = p3.shape
    return pl.pallas_call(
        functools.partial(_gate_body, heads=heads, lane0=lane0),
        grid=(b, lp // GATE_TB),
        in_specs=[
            pl.BlockSpec((1, GATE_TB, LANES), lambda i, c: (i, c, lane_blk)),
            pl.BlockSpec((1, LANES), lambda i, c: (0, 0)),
        ],
        out_specs=pl.BlockSpec((1, heads, GATE_TB, LANES), lambda i, c: (i, 0, c, 0)),
        out_shape=jax.ShapeDtypeStruct((b, heads, lp, LANES), BF16),
        scratch_shapes=[pltpu.VMEM((1, LANES), F32)],
        compiler_params=_params("parallel", "arbitrary"),
    )(p3, bf_row)


def _attn_body(q_ref, k_ref, vt_ref, cp_ref, o_ref, m_ref, l_ref, acc_ref, *, heads):
    qi = pl.program_id(1)
    m_ref[...] = jnp.full_like(m_ref, NEG_INF)
    l_ref[...] = jnp.zeros_like(l_ref)
    acc_ref[...] = jnp.zeros_like(acc_ref)
    hs = [slice(g * FOX_HEAD, (g + 1) * FOX_HEAD) for g in range(heads)]
    gs = range(heads)
    neg_ones = -(lax.broadcasted_iota(jnp.int32, (ATT_T, LANES), 1) < 3).astype(BF16)
    q_aug = [jnp.concatenate([q_ref[0, :, hs[g]], neg_ones], axis=1) for g in gs]

    def block(j, masked):
        off = pl.multiple_of(j * ATT_T, ATT_T)
        s = [lax.dot_general(
            jnp.concatenate([k_ref[0, pl.ds(off, ATT_T), hs[g]], cp_ref[0, g, pl.ds(off, ATT_T), :]], axis=1),
            q_aug[g], _NT, preferred_element_type=F32) for g in gs]
        if masked:
            kpos = lax.broadcasted_iota(jnp.int32, (ATT_T, ATT_T), 0)
            qpos = lax.broadcasted_iota(jnp.int32, (ATT_T, ATT_T), 1)
            s = [jnp.where(kpos <= qpos, x, NEG_INF) for x in s]
        m_prev = [m_ref[g] for g in gs]
        m_new = [jnp.maximum(m_prev[g], jnp.max(s[g], axis=0, keepdims=True)) for g in gs]
        p = [jnp.exp2(s[g] - m_new[g]) for g in gs]
        alpha = [jnp.exp2(m_prev[g] - m_new[g]) for g in gs]
        pv = [jnp.dot(vt_ref[0, hs[g], pl.ds(off, ATT_T)], p[g].astype(BF16),
                      preferred_element_type=F32) for g in gs]
        for g in gs:
            l_ref[g] = alpha[g] * l_ref[g] + jnp.sum(p[g], axis=0, keepdims=True)
            acc_ref[hs[g], :] = alpha[g] * acc_ref[hs[g], :] + pv[g]
            m_ref[g] = m_new[g]

    def body(j, carry):
        block(j, False)
        return carry

    lax.fori_loop(0, qi, body, 0)
    block(qi, True)
    for g in gs:
        o_ref[0, :, hs[g]] = (acc_ref[hs[g], :] / l_ref[g]).T


def _attention(qkv3, vt3, cp, width):
    b, lp, _ = qkv3.shape
    heads = width // FOX_HEAD
    resident = pl.Buffered(1)
    return pl.pallas_call(
        functools.partial(_attn_body, heads=heads),
        grid=(b, lp // ATT_T),
        in_specs=[
            pl.BlockSpec((1, ATT_T, width), lambda i, q: (i, q, 0)),
            pl.BlockSpec((1, lp, width), lambda i, q: (i, 0, 1), pipeline_mode=resident),
            pl.BlockSpec((1, width, lp), lambda i, q: (i, 0, 0), pipeline_mode=resident),
            pl.BlockSpec((1, heads, lp, LANES), lambda i, q: (i, 0, 0, 0), pipeline_mode=resident),
        ],
        out_specs=pl.BlockSpec((1, ATT_T, width), lambda i, q: (i, q, 0)),
        out_shape=jax.ShapeDtypeStruct((b, lp, width), F32),
        scratch_shapes=[
            pltpu.VMEM((heads, 1, ATT_T), F32),
            pltpu.VMEM((heads, 1, ATT_T), F32),
            pltpu.VMEM((width, ATT_T), F32),
        ],
        compiler_params=_params("parallel", "arbitrary"),
    )(qkv3, qkv3, vt3, cp)


def _out_body(h_ref, yr_ref, yf_ref, gn_ref, w1_ref, w2_ref, o_ref):
    yf = _rms_scale(yf_ref[...], gn_ref[...])
    acc = jnp.dot(yr_ref[...], w1_ref[...], preferred_element_type=F32)
    acc = acc + jnp.dot(yf.astype(BF16), w2_ref[...], preferred_element_type=F32)
    o_ref[...] = h_ref[...] + acc


def _out_proj(h, y_rwkv, y_fox, fox_gain, w_out):
    t, d = h.shape
    w1 = y_rwkv.shape[1]
    w2 = y_fox.shape[1]
    return pl.pallas_call(
        _out_body,
        grid=(t // OUT_TM,),
        in_specs=[
            pl.BlockSpec((OUT_TM, d), lambda i: (i, 0)),
            pl.BlockSpec((OUT_TM, w1), lambda i: (i, 0)),
            pl.BlockSpec((OUT_TM, w2), lambda i: (i, 0)),
            pl.BlockSpec((1, w2), lambda i: (0, 0)),
            pl.BlockSpec((w1, d), lambda i: (0, 0)),
            pl.BlockSpec((w2, d), lambda i: (1, 0)),
        ],
        out_specs=pl.BlockSpec((OUT_TM, d), lambda i: (i, 0)),
        out_shape=jax.ShapeDtypeStruct((t, d), F32),
        compiler_params=_params("parallel"),
    )(h, y_rwkv, y_fox, fox_gain, w_out, w_out)


def _norm_body(h_ref, g_ref, o_ref):
    o_ref[...] = _rms_scale(h_ref[...], g_ref[...])


def _final_norm(h, gain):
    t, d = h.shape
    return pl.pallas_call(
        _norm_body,
        grid=(t // OUT_TM,),
        in_specs=[pl.BlockSpec((OUT_TM, d), lambda i: (i, 0)), pl.BlockSpec((1, d), lambda i: (0, 0))],
        out_specs=pl.BlockSpec((OUT_TM, d), lambda i: (i, 0)),
        out_shape=jax.ShapeDtypeStruct((t, d), F32),
        compiler_params=_params("parallel"),
    )(h, gain)


def _pad_rows(w, rows):
    return jnp.pad(w, ((0, rows - w.shape[0]), (0, 0)))


def _pack_w_rwkv(w_in, width, heads):
    d = w_in.shape[0]
    c3 = 3 * width
    c4, c5, c6 = c3 + W_LORA, c3 + W_LORA + A_LORA, c3 + W_LORA + A_LORA + G_LORA
    lora = jnp.zeros((d, LORA_W), w_in.dtype)
    lora = lora.at[:, 0:W_LORA].set(w_in[:, c3:c4])
    lora = lora.at[:, LANES:LANES + A_LORA].set(w_in[:, c4:c5])
    lora = lora.at[:, 2 * LANES:2 * LANES + G_LORA].set(w_in[:, c5:c6])
    lora = lora.at[:, F_LANE:F_LANE + heads].set(w_in[:, c6 + c3:c6 + c3 + heads])
    return jnp.concatenate([w_in[:, :c3], lora], axis=1).astype(BF16)


def _pack_mu_lora(mu, width):
    c3 = 3 * width
    c4, c5, c6 = c3 + W_LORA, c3 + W_LORA + A_LORA, c3 + W_LORA + A_LORA + G_LORA
    out = jnp.zeros((LORA_W,), mu.dtype)
    out = out.at[0:W_LORA].set(mu[c3:c4])
    out = out.at[LANES:LANES + A_LORA].set(mu[c4:c5])
    out = out.at[2 * LANES:2 * LANES + G_LORA].set(mu[c5:c6])
    return out[None, :]


def kernel(x, meta_tokens, ffn1_norm, ffn1_w_gu, ffn1_w_down, mix_norm, w_in, rwkv_mu, rwkv_w0,
           rwkv_w_up, rwkv_a0, rwkv_a_up, rwkv_g_up, rwkv_k_k, rwkv_k_a, rwkv_r_k, rwkv_lnx_w,
           rwkv_lnx_b, fox_b_f, fox_out_norm, w_out, ffn2_norm, ffn2_w_gu, ffn2_w_down, final_norm):
    b, seq, d = x.shape
    depth = w_in.shape[0]
    width = rwkv_w0.shape[1]
    heads = fox_b_f.shape[1]
    assert rwkv_g_up.shape[1] == G_LORA and G_LORA <= 2 * LANES
    assert F_LANE >= 2 * LANES + G_LORA and F_LANE + heads <= LORA_W and F_LANE % LANES + heads <= LANES
    l = N_META + seq
    lp = -(-l // TIME_ALIGN) * TIME_ALIGN
    t = b * lp
    assert t % PROJ_TM == 0 and t % FFN_TM == 0 and lp % RWKV_ROWS == 0 and RWKV_ROWS % CHUNK == 0
    assert width % PROJ_TN == 0

    meta = jnp.broadcast_to(meta_tokens.astype(x.dtype)[None], (b, N_META, d))
    h = jnp.concatenate([meta, x, jnp.zeros((b, lp - l, d), x.dtype)], axis=1).reshape(t, d)

    row = lambda vct: vct.astype(F32)[None, :]
    c6 = 3 * width + W_LORA + A_LORA + G_LORA
    f_blk = (3 * width + F_LANE) // LANES
    f_lane0 = F_LANE % LANES
    for i in range(depth):
        h = _ffn(h, row(ffn1_norm[i]), ffn1_w_gu[i].astype(BF16), ffn1_w_down[i].astype(BF16))
        gain = row(mix_norm[i])
        p_rwkv = _in_proj(h, gain, _pack_w_rwkv(w_in[i], width, heads), F32)
        qkv = _in_proj(h, gain, w_in[i, :, c6:c6 + 3 * width].astype(BF16), BF16,
                       scaled_blocks=width // PROJ_TN, scale=FOX_HEAD ** -0.5 * LOG2E)
        p3 = p_rwkv.reshape(b, lp, p_rwkv.shape[1])
        prm = dict(
            mu_rkv=row(rwkv_mu[i, :3 * width]), mu_lora=_pack_mu_lora(rwkv_mu[i], width),
            w0=row(rwkv_w0[i]), w_up=_pad_rows(rwkv_w_up[i], LANES).astype(BF16),
            a0=row(rwkv_a0[i]), a_up=_pad_rows(rwkv_a_up[i], LANES).astype(BF16),
            g_up=_pad_rows(rwkv_g_up[i], 2 * LANES).astype(BF16),
            k_k=row(rwkv_k_k[i]), k_a=row(rwkv_k_a[i]), r_k=row(rwkv_r_k[i].reshape(-1)),
            lnx_w=row(rwkv_lnx_w[i]), lnx_b=row(rwkv_lnx_b[i]))
        y_rwkv = _rwkv(p3, prm, width)
        bf_row = jnp.zeros((1, LANES), F32).at[0, f_lane0:f_lane0 + heads].set(fox_b_f[i])
        cp = _gate_cumsum(p3, bf_row, f_blk, heads, f_lane0)
        qkv3 = qkv.reshape(b, lp, 3 * width)
        vt3 = jnp.transpose(qkv3[:, :, 2 * width:], (0, 2, 1))
        y_fox = _attention(qkv3, vt3, cp, width)
        h = _out_proj(h, y_rwkv.reshape(t, width), y_fox.reshape(t, width), row(fox_out_norm[i]),
                      w_out[i].astype(BF16))
        h = _ffn(h, row(ffn2_norm[i]), ffn2_w_gu[i].astype(BF16), ffn2_w_down[i].astype(BF16))
    out = _final_norm(h, row(final_norm))
    return out.reshape(b, lp, d)[:, N_META:l]
```

```python
import functools
import math

import jax
import jax.numpy as jnp
from jax import lax
from jax.experimental import pallas as pl
from jax.experimental.pallas import tpu as pltpu

F32 = jnp.float32
BF16 = jnp.bfloat16

N_META = 16
RWKV_HEAD = 64
FOX_HEAD = 128
W_LORA = 64
A_LORA = 64
G_LORA = 160
NORM_EPS = 1e-6
LNX_EPS = 64e-5
L2_EPS = 1e-12
NEG_INF = -1e30
LOG2E = math.log2(math.e)

LANES = 128
VMEM_LIMIT_BYTES = 56 * 1024 * 1024

CHUNK = 64
RWKV_ROWS = 192
PAIR = 2 * RWKV_HEAD
TIME_ALIGN = 384
ATT_T = 384
GATE_TB = 384
FFN_TM = 768
FFN_TF = 512
NORM_TM = 512
PROJ_TM = 1408
PROJ_TN = 512
OUT_TM = 512
LORA_W = 512
F_LANE = 448

_NT = (((1,), (1,)), ((), ()))


def _rms_scale(x, gain):
    ms = jnp.mean(x * x, axis=-1, keepdims=True)
    return x * lax.rsqrt(ms + NORM_EPS) * gain


def _dot(a, b):
    return jnp.dot(a.astype(BF16), b.astype(BF16), preferred_element_type=F32)


def _dot_nt(a, b):
    return lax.dot_general(a.astype(BF16), b.astype(BF16), _NT, preferred_element_type=F32)


def _split3(x):
    hi = x.astype(BF16)
    r1 = x - hi.astype(F32)
    mid = r1.astype(BF16)
    lo = (r1 - mid.astype(F32)).astype(BF16)
    return hi, mid, lo


def _dot_exact_lhs(m_bf16, x):
    hi, mid, lo = _split3(x)
    acc = jnp.dot(m_bf16, lo, preferred_element_type=F32)
    acc = acc + jnp.dot(m_bf16, mid, preferred_element_type=F32)
    return acc + jnp.dot(m_bf16, hi, preferred_element_type=F32)


def _params(*sem):
    return pltpu.CompilerParams(dimension_semantics=sem, vmem_limit_bytes=VMEM_LIMIT_BYTES)


def _ffn_body(h_ref, g_ref, wg_ref, wu_ref, wd_ref, o_ref, u_ref):
    f = pl.program_id(1)

    @pl.when(f == 0)
    def _():
        x = h_ref[...]
        u_ref[...] = _rms_scale(x, g_ref[...]).astype(BF16)
        o_ref[...] = x

    u = u_ref[...]
    gate = jnp.dot(u, wg_ref[0], preferred_element_type=F32)
    up = jnp.dot(u, wu_ref[0], preferred_element_type=F32)
    act = (0.5 * gate) * jax.nn.sigmoid(gate) * up
    o_ref[...] += jnp.dot(act.astype(BF16), wd_ref[0], preferred_element_type=F32)


def _ffn(h, gain, w_gu, w_down, layer):
    t, d = h.shape
    ff = w_down.shape[1]
    nf = ff // FFN_TF
    return pl.pallas_call(
        _ffn_body,
        grid=(t // FFN_TM, nf),
        in_specs=[
            pl.BlockSpec((FFN_TM, d), lambda i, f: (i, 0)),
            pl.BlockSpec((1, d), lambda i, f: (0, 0)),
            pl.BlockSpec((1, d, FFN_TF), lambda i, f: (layer, 0, f)),
            pl.BlockSpec((1, d, FFN_TF), lambda i, f: (layer, 0, f + nf)),
            pl.BlockSpec((1, FFN_TF, d), lambda i, f: (layer, f, 0)),
        ],
        out_specs=pl.BlockSpec((FFN_TM, d), lambda i, f: (i, 0)),
        out_shape=jax.ShapeDtypeStruct((t, d), F32),
        scratch_shapes=[pltpu.VMEM((FFN_TM, d), BF16)],
        compiler_params=_params("parallel", "arbitrary"),
    )(h, gain, w_gu, w_gu, w_down)


def _proj_body(h_ref, g_ref, w_ref, o_ref, ol_ref, u_ref, *, n_main, scaled_blocks, scale):
    j = pl.program_id(1)

    @pl.when(j == 0)
    def _():
        u_ref[...] = _rms_scale(h_ref[...], g_ref[...]).astype(BF16)

    acc = jnp.dot(u_ref[...], w_ref[...], preferred_element_type=F32)

    @pl.when(j < n_main)
    def _():
        o_ref[...] = (acc * jnp.where(j < scaled_blocks, scale, 1.0)).astype(o_ref.dtype)

    @pl.when(j == n_main)
    def _():
        ol_ref[...] = acc


def _in_proj(h, gain, w, scaled_blocks, scale):
    t, d = h.shape
    n = w.shape[1]
    n_main = n // PROJ_TN - 1
    return pl.pallas_call(
        functools.partial(_proj_body, n_main=n_main, scaled_blocks=scaled_blocks, scale=scale),
        grid=(t // PROJ_TM, n_main + 1),
        in_specs=[
            pl.BlockSpec((PROJ_TM, d), lambda i, j: (i, 0)),
            pl.BlockSpec((1, d), lambda i, j: (0, 0)),
            pl.BlockSpec((d, PROJ_TN), lambda i, j: (0, j)),
        ],
        out_specs=[
            pl.BlockSpec((PROJ_TM, PROJ_TN), lambda i, j: (i, jnp.minimum(j, n_main - 1))),
            pl.BlockSpec((PROJ_TM, PROJ_TN), lambda i, j: (i, 0)),
        ],
        out_shape=[jax.ShapeDtypeStruct((t, n_main * PROJ_TN), BF16),
                   jax.ShapeDtypeStruct((t, PROJ_TN), F32)],
        scratch_shapes=[pltpu.VMEM((PROJ_TM, d), BF16)],
        compiler_params=_params("parallel", "arbitrary"),
    )(h, gain, w)


def _rwkv_body(rkv_ref, lora_ref, mu_rkv_ref, mu_lora_ref, w0_ref, wup_ref, a0_ref, aup_ref,
               gup_ref, kk_ref, ka_ref, rk_ref, lnw_ref, lnb_ref, o_ref,
               carry_rkv, carry_lora, st_ref, *, width):
    rows = RWKV_ROWS
    n_ch = rows // CHUNK
    c = pl.program_id(1)

    @pl.when(c == 0)
    def _():
        carry_rkv[...] = jnp.zeros_like(carry_rkv)
        carry_lora[...] = jnp.zeros_like(carry_lora)
        st_ref[...] = jnp.zeros_like(st_ref)

    row = lax.broadcasted_iota(jnp.int32, (rows, 1), 0)

    def shift(p, carry_ref, mu):
        prev = jnp.where(row == 0, carry_ref[...], pltpu.roll(p, 1, 0))
        carry_ref[...] = p[rows - 1:rows, :]
        return p + (prev - p) * mu

    pf = shift(rkv_ref[0].astype(F32), carry_rkv, mu_rkv_ref[...])
    lf = shift(lora_ref[0], carry_lora, mu_lora_ref[...])
    r = pf[:, :width]
    k = pf[:, width:2 * width]
    v = pf[:, 2 * width:]
    wd = lf[:, 0:LANES]
    ad = lf[:, LANES:2 * LANES]
    gd = lf[:, 2 * LANES:4 * LANES]

    z = w0_ref[...] + _dot(jnp.tanh(wd), wup_ref[...])
    nz = -z
    softplus = jnp.maximum(nz, 0.0) + jnp.log1p(jnp.exp(-jnp.abs(nz)))
    logw = -jnp.exp(-softplus - 0.5)
    a = jax.nn.sigmoid(a0_ref[...] + _dot(ad, aup_ref[...]))
    g = _dot(jax.nn.sigmoid(gd), gup_ref[...])

    ci = lax.broadcasted_iota(jnp.int32, (rows, rows), 0)
    cj = lax.broadcasted_iota(jnp.int32, (rows, rows), 1)
    ltri = ((cj <= ci) & ((ci // CHUNK) == (cj // CHUNK))).astype(BF16)
    cs = _dot_exact_lhs(ltri, logw)

    ii = lax.broadcasted_iota(jnp.int32, (PAIR, PAIR), 0)
    jj = lax.broadcasted_iota(jnp.int32, (PAIR, PAIR), 1)
    same_head = (ii // RWKV_HEAD) == (jj // RWKV_HEAD)
    mask_strict = same_head & (jj < ii)
    mask_incl = same_head & (jj <= ii)
    eye = ii == jj
    seg_ones = same_head.astype(BF16)
    lane = lax.broadcasted_iota(jnp.int32, (CHUNK, PAIR), 1)
    head0 = lane < RWKV_HEAD

    def stack(x):
        return jnp.concatenate([jnp.where(head0, x, 0.0), jnp.where(head0, 0.0, x)], axis=0)

    def both(x):
        return jnp.concatenate([x, x], axis=0)

    pairs = range(width // PAIR)
    sls = [slice(hp * PAIR, (hp + 1) * PAIR) for hp in pairs]
    chunks = [slice(ch * CHUNK, (ch + 1) * CHUNK) for ch in range(n_ch)]
    units = [(ch, p) for ch in range(n_ch) for p in pairs]

    kk = [k[:, sl] * kk_ref[:, sl] for sl in sls]
    ss = [_dot(x * x, seg_ones) for x in kk]
    kk = [x / jnp.maximum(jnp.sqrt(s), L2_EPS) for x, s in zip(kk, ss)]
    k2 = [k[:, sl] * (1.0 + (a[:, sl] - 1.0) * ka_ref[:, sl]) for sl in sls]
    bv = [x * a[:, sl] for x, sl in zip(kk, sls)]
    e_neg = [jnp.exp(-cs[:, sl]) for sl in sls]
    at_ = [-kk[p] * jnp.exp(cs[:, sls[p]] - logw[:, sls[p]]) for p in pairs]
    rt_ = [r[:, sl] * jnp.exp(cs[:, sl]) for sl in sls]
    bt_ = [bv[p] * e_neg[p] for p in pairs]
    kt_ = [k2[p] * e_neg[p] for p in pairs]

    last = [cs[chunks[ch]][CHUNK - 1:CHUNK, sls[p]] for ch, p in units]
    e_end = [jnp.exp(last[u] - cs[chunks[ch], sls[p]]) for u, (ch, p) in enumerate(units)]
    xa = [stack(at_[p][chunks[ch]]) for ch, p in units]
    xr = [stack(rt_[p][chunks[ch]]) for ch, p in units]
    xbh_t = [stack(bv[p][chunks[ch]] * e_end[u]).T for u, (ch, p) in enumerate(units)]
    xkh_t = [stack(k2[p][chunks[ch]] * e_end[u]).T for u, (ch, p) in enumerate(units)]
    vst = [stack(v[chunks[ch], sls[p]]) for ch, p in units]

    big = [_dot_nt(jnp.concatenate([xa[u], xr[u]], axis=0),
                   jnp.concatenate([both(bt_[p][chunks[ch]]), both(kt_[p][chunks[ch]])], axis=0))
           for u, (ch, p) in enumerate(units)]
    big = [x.astype(BF16) for x in big]
    m_strict, m_incl = mask_strict.astype(BF16), mask_incl.astype(BF16)
    a_ab = [x[:PAIR, :PAIR] for x in big]
    a_ak = [x[:PAIR, PAIR:] * m_strict for x in big]
    a_rb = [x[PAIR:, :PAIR] * m_incl for x in big]
    a_rk = [x[PAIR:, PAIR:] * m_incl for x in big]

    lvl1 = (((ii ^ jj) == 1) & (jj < ii)).astype(BF16)
    eye_bf = eye.astype(BF16)
    tinv = [eye_bf + x * lvl1 for x in a_ab]
    s = 2
    while s < RWKV_HEAD:
        e_mask = (((ii // (2 * s)) == (jj // (2 * s))) & (((ii // s) % 2) == 1)
                  & (((jj // s) % 2) == 0)).astype(BF16)
        half = [_dot(t_, x * e_mask).astype(BF16) for t_, x in zip(tinv, a_ab)]
        tinv = [t_ + _dot(h_, t_).astype(BF16) for t_, h_ in zip(tinv, half)]
        s *= 2

    n_u = range(len(units))
    w_av = [_dot(a_ak[u], vst[u]) for u in n_u]
    x = [_dot(tinv[u], jnp.concatenate([xa[u], w_av[u]], axis=1)) for u in n_u]
    mg = [_dot(xbh_t[u], x[u]) for u in n_u]
    kv = [_dot(xkh_t[u], vst[u]) for u in n_u]
    ry = [_dot(a_rb[u], x[u]) for u in n_u]
    rkv_ = [_dot(a_rk[u], vst[u]) for u in n_u]
    rh = [xr[u] + ry[u][:, :PAIR] for u in n_u]
    y0 = [ry[u][:, PAIR:] + rkv_[u] for u in n_u]
    mm = [mg[u][:, :PAIR] + jnp.where(eye, jnp.exp(last[u]), 0.0) for u in n_u]
    gg = [mg[u][:, PAIR:] + kv[u] for u in n_u]

    st = [st_ref[p] for p in pairs]
    yst = []
    for ch in range(n_ch):
        base = ch * len(pairs)
        yst.append([_dot(rh[base + p], st[p]) + y0[base + p] for p in pairs])
        st = [_dot(mm[base + p], st[p]) + gg[base + p] for p in pairs]
    for p in pairs:
        st_ref[p] = st[p]
    y = [jnp.concatenate([yst[ch][p][:CHUNK] + yst[ch][p][CHUNK:] for ch in range(n_ch)], axis=0)
         for p in pairs]

    inv_n = 1.0 / RWKV_HEAD
    mean = [_dot(x_, seg_ones) * inv_n for x_ in y]
    dlt = [x_ - m_ for x_, m_ in zip(y, mean)]
    var = [_dot(x_ * x_, seg_ones) * inv_n for x_ in dlt]
    bonus = [_dot(r[:, sl] * k2[p] * rk_ref[:, sl], seg_ones) * v[:, sl] for p, sl in zip(pairs, sls)]
    for p, sl in zip(pairs, sls):
        yn = dlt[p] * lax.rsqrt(var[p] + LNX_EPS) * lnw_ref[:, sl] + lnb_ref[:, sl]
        o_ref[0, :, sl] = ((yn + bonus[p]) * g[:, sl]).astype(o_ref.dtype)


def _rwkv(pm3, pl3, prm, width):
    b, lp, _ = pm3.shape
    n_rkv = 3 * width
    vec = lambda n: pl.BlockSpec((1, n), lambda i, c: (0, 0))
    mat = lambda k: pl.BlockSpec((k, width), lambda i, c: (0, 0))
    return pl.pallas_call(
        functools.partial(_rwkv_body, width=width),
        grid=(b, lp // RWKV_ROWS),
        in_specs=[
            pl.BlockSpec((1, RWKV_ROWS, n_rkv), lambda i, c: (i, c, 1)),
            pl.BlockSpec((1, RWKV_ROWS, LORA_W), lambda i, c: (i, c, 0)),
            vec(n_rkv), vec(LORA_W), vec(width), mat(LANES), vec(width), mat(LANES),
            mat(2 * LANES), vec(width), vec(width), vec(width), vec(width), vec(width),
        ],
        out_specs=pl.BlockSpec((1, RWKV_ROWS, width), lambda i, c: (i, c, 0)),
        out_shape=jax.ShapeDtypeStruct((b, lp, width), BF16),
        scratch_shapes=[
            pltpu.VMEM((1, n_rkv), F32),
            pltpu.VMEM((1, LORA_W), F32),
            pltpu.VMEM((width // PAIR, PAIR, PAIR), F32),
        ],
        compiler_params=_params("parallel", "arbitrary"),
    )(pm3, pl3, prm["mu_rkv"], prm["mu_lora"], prm["w0"], prm["w_up"], prm["a0"], prm["a_up"],
      prm["g_up"], prm["k_k"], prm["k_a"], prm["r_k"], prm["lnx_w"], prm["lnx_b"])


def _gate_body(x_ref, bf_ref, o_ref, carry_ref, *, heads, lane0):
    @pl.when(pl.program_id(1) == 0)
    def _():
        carry_ref[...] = jnp.zeros_like(carry_ref)

    zz = x_ref[0] + bf_ref[...]
    log_f = jnp.minimum(zz, 0.0) - jnp.log1p(jnp.exp(-jnp.abs(zz)))
    ci = lax.broadcasted_iota(jnp.int32, (GATE_TB, GATE_TB), 0)
    cj = lax.broadcasted_iota(jnp.int32, (GATE_TB, GATE_TB), 1)
    csum = _dot_exact_lhs((cj <= ci).astype(BF16), log_f) + carry_ref[...]
    carry_ref[...] = csum[GATE_TB - 1:GATE_TB, :]
    pieces = _split3(csum * LOG2E)
    src = lax.broadcasted_iota(jnp.int32, (LANES, LANES), 0)
    dst = lax.broadcasted_iota(jnp.int32, (LANES, LANES), 1)
    for g in range(heads):
        acc = jnp.zeros((GATE_TB, LANES), F32)
        for i, piece in enumerate(pieces):
            move = ((src == lane0 + g) & (dst == i)).astype(BF16)
            acc = acc + jnp.dot(piece, move, preferred_element_type=F32)
        o_ref[0, g] = acc.astype(BF16)


def _gate_cumsum(p3, bf_row, lane_blk, heads, lane0):
    b, lp, _ = p3.shape
    return pl.pallas_call(
        functools.partial(_gate_body, heads=heads, lane0=lane0),
        grid=(b, lp // GATE_TB),
        in_specs=[
            pl.BlockSpec((1, GATE_TB, LANES), lambda i, c: (i, c, lane_blk)),
            pl.BlockSpec((1, LANES), lambda i, c: (0, 0)),
        ],
        out_specs=pl.BlockSpec((1, heads, GATE_TB, LANES), lambda i, c: (i, 0, c, 0)),
        out_shape=jax.ShapeDtypeStruct((b, heads, lp, LANES), BF16),
        scratch_shapes=[pltpu.VMEM((1, LANES), F32)],
        compiler_params=_params("parallel", "arbitrary"),
    )(p3, bf_row)


def _attn_body(q_ref, k_ref, vt_ref, cp_ref, o_ref, m_ref, l_ref, acc_ref, *, heads):
    qi = pl.program_id(1)
    m_ref[...] = jnp.full_like(m_ref, NEG_INF)
    l_ref[...] = jnp.zeros_like(l_ref)
    acc_ref[...] = jnp.zeros_like(acc_ref)
    hs = [slice(g * FOX_HEAD, (g + 1) * FOX_HEAD) for g in range(heads)]
    gs = range(heads)
    neg_ones = -(lax.broadcasted_iota(jnp.int32, (ATT_T, LANES), 1) < 3).astype(BF16)
    q_aug = [jnp.concatenate([q_ref[0, :, hs[g]], neg_ones], axis=1) for g in gs]

    def block(j, masked):
        off = pl.multiple_of(j * ATT_T, ATT_T)
        s = [lax.dot_general(
            jnp.concatenate([k_ref[0, pl.ds(off, ATT_T), hs[g]], cp_ref[0, g, pl.ds(off, ATT_T), :]], axis=1),
            q_aug[g], _NT, preferred_element_type=F32) for g in gs]
        if masked:
            kpos = lax.broadcasted_iota(jnp.int32, (ATT_T, ATT_T), 0)
            qpos = lax.broadcasted_iota(jnp.int32, (ATT_T, ATT_T), 1)
            s = [jnp.where(kpos <= qpos, x, NEG_INF) for x in s]
        m_prev = [m_ref[g] for g in gs]
        m_new = [jnp.maximum(m_prev[g], jnp.max(s[g], axis=0, keepdims=True)) for g in gs]
        p = [jnp.exp2(s[g] - m_new[g]) for g in gs]
        alpha = [jnp.exp2(m_prev[g] - m_new[g]) for g in gs]
        pv = [jnp.dot(vt_ref[0, hs[g], pl.ds(off, ATT_T)], p[g].astype(BF16),
                      preferred_element_type=F32) for g in gs]
        for g in gs:
            l_ref[g] = alpha[g] * l_ref[g] + jnp.sum(p[g], axis=0, keepdims=True)
            acc_ref[hs[g], :] = alpha[g] * acc_ref[hs[g], :] + pv[g]
            m_ref[g] = m_new[g]

    def body(j, carry):
        block(j, False)
        return carry

    lax.fori_loop(0, qi, body, 0)
    block(qi, True)
    for g in gs:
        o_ref[0, :, hs[g]] = (acc_ref[hs[g], :] / l_ref[g]).T


def _attention(qkv3, vt3, cp, width):
    b, lp, _ = qkv3.shape
    heads = width // FOX_HEAD
    resident = pl.Buffered(1)
    return pl.pallas_call(
        functools.partial(_attn_body, heads=heads),
        grid=(b, lp // ATT_T),
        in_specs=[
            pl.BlockSpec((1, ATT_T, width), lambda i, q: (i, q, 0)),
            pl.BlockSpec((1, lp, width), lambda i, q: (i, 0, 1), pipeline_mode=resident),
            pl.BlockSpec((1, width, lp), lambda i, q: (i, 0, 0), pipeline_mode=resident),
            pl.BlockSpec((1, heads, lp, LANES), lambda i, q: (i, 0, 0, 0), pipeline_mode=resident),
        ],
        out_specs=pl.BlockSpec((1, ATT_T, width), lambda i, q: (i, q, 0)),
        out_shape=jax.ShapeDtypeStruct((b, lp, width), F32),
        scratch_shapes=[
            pltpu.VMEM((heads, 1, ATT_T), F32),
            pltpu.VMEM((heads, 1, ATT_T), F32),
            pltpu.VMEM((width, ATT_T), F32),
        ],
        compiler_params=_params("parallel", "arbitrary"),
    )(qkv3, qkv3, vt3, cp)


def _out_body(h_ref, yr_ref, yf_ref, gn_ref, w1_ref, w2_ref, o_ref):
    yf = _rms_scale(yf_ref[...], gn_ref[...])
    acc = jnp.dot(yr_ref[...], w1_ref[0], preferred_element_type=F32)
    acc = acc + jnp.dot(yf.astype(BF16), w2_ref[0], preferred_element_type=F32)
    o_ref[...] = h_ref[...] + acc


def _out_proj(h, y_rwkv, y_fox, fox_gain, w_out, layer):
    t, d = h.shape
    w1 = y_rwkv.shape[1]
    w2 = y_fox.shape[1]
    return pl.pallas_call(
        _out_body,
        grid=(t // OUT_TM,),
        in_specs=[
            pl.BlockSpec((OUT_TM, d), lambda i: (i, 0)),
            pl.BlockSpec((OUT_TM, w1), lambda i: (i, 0)),
            pl.BlockSpec((OUT_TM, w2), lambda i: (i, 0)),
            pl.BlockSpec((1, w2), lambda i: (0, 0)),
            pl.BlockSpec((1, w1, d), lambda i: (layer, 0, 0)),
            pl.BlockSpec((1, w2, d), lambda i: (layer, 1, 0)),
        ],
        out_specs=pl.BlockSpec((OUT_TM, d), lambda i: (i, 0)),
        out_shape=jax.ShapeDtypeStruct((t, d), F32),
        compiler_params=_params("parallel"),
    )(h, y_rwkv, y_fox, fox_gain, w_out, w_out)


def _norm_body(a_ref, b_ref, g_ref, o_ref):
    x = jnp.concatenate([a_ref[0, N_META:, :], b_ref[0]], axis=0)
    o_ref[0] = _rms_scale(x, g_ref[...])


def _final_norm(h3, gain, seq):
    b, _, d = h3.shape
    per = NORM_TM // N_META
    return pl.pallas_call(
        _norm_body,
        grid=(b, seq // NORM_TM),
        in_specs=[
            pl.BlockSpec((1, NORM_TM, d), lambda i, r: (i, r, 0)),
            pl.BlockSpec((1, N_META, d), lambda i, r: (i, (r + 1) * per, 0)),
            pl.BlockSpec((1, d), lambda i, r: (0, 0)),
        ],
        out_specs=pl.BlockSpec((1, NORM_TM, d), lambda i, r: (i, r, 0)),
        out_shape=jax.ShapeDtypeStruct((b, seq, d), F32),
        compiler_params=_params("parallel", "parallel"),
    )(h3, h3, gain)


def _pad_rows(w, rows):
    return jnp.pad(w, ((0, rows - w.shape[0]), (0, 0)))


def _pack_w_in(w_in, width, heads):
    d = w_in.shape[0]
    c3 = 3 * width
    c4, c5, c6 = c3 + W_LORA, c3 + W_LORA + A_LORA, c3 + W_LORA + A_LORA + G_LORA
    lora = jnp.zeros((d, LORA_W), w_in.dtype)
    lora = lora.at[:, 0:W_LORA].set(w_in[:, c3:c4])
    lora = lora.at[:, LANES:LANES + A_LORA].set(w_in[:, c4:c5])
    lora = lora.at[:, 2 * LANES:2 * LANES + G_LORA].set(w_in[:, c5:c6])
    lora = lora.at[:, F_LANE:F_LANE + heads].set(w_in[:, c6 + c3:c6 + c3 + heads])
    return jnp.concatenate([w_in[:, c6:c6 + c3], w_in[:, :c3], lora], axis=1).astype(BF16)


def _pack_mu_lora(mu, width):
    c3 = 3 * width
    c4, c5, c6 = c3 + W_LORA, c3 + W_LORA + A_LORA, c3 + W_LORA + A_LORA + G_LORA
    out = jnp.zeros((LORA_W,), mu.dtype)
    out = out.at[0:W_LORA].set(mu[c3:c4])
    out = out.at[LANES:LANES + A_LORA].set(mu[c4:c5])
    out = out.at[2 * LANES:2 * LANES + G_LORA].set(mu[c5:c6])
    return out[None, :]


def kernel(x, meta_tokens, ffn1_norm, ffn1_w_gu, ffn1_w_down, mix_norm, w_in, rwkv_mu, rwkv_w0,
           rwkv_w_up, rwkv_a0, rwkv_a_up, rwkv_g_up, rwkv_k_k, rwkv_k_a, rwkv_r_k, rwkv_lnx_w,
           rwkv_lnx_b, fox_b_f, fox_out_norm, w_out, ffn2_norm, ffn2_w_gu, ffn2_w_down, final_norm):
    b, seq, d = x.shape
    depth = w_in.shape[0]
    width = rwkv_w0.shape[1]
    heads = fox_b_f.shape[1]
    assert rwkv_g_up.shape[1] == G_LORA and G_LORA <= 2 * LANES
    assert F_LANE >= 2 * LANES + G_LORA and F_LANE + heads <= LORA_W and F_LANE % LANES + heads <= LANES
    l = N_META + seq
    lp = -(-l // TIME_ALIGN) * TIME_ALIGN
    t = b * lp
    assert t % PROJ_TM == 0 and t % FFN_TM == 0 and lp % RWKV_ROWS == 0 and RWKV_ROWS % CHUNK == 0
    assert width % PROJ_TN == 0 and LORA_W == PROJ_TN

    meta = jnp.broadcast_to(meta_tokens.astype(x.dtype)[None], (b, N_META, d))
    h = jnp.concatenate([meta, x, jnp.zeros((b, lp - l, d), x.dtype)], axis=1).reshape(t, d)

    row = lambda vct: vct.astype(F32)[None, :]
    f_blk = F_LANE // LANES
    f_lane0 = F_LANE % LANES
    wgu1, wdn1 = ffn1_w_gu.astype(BF16), ffn1_w_down.astype(BF16)
    wgu2, wdn2 = ffn2_w_gu.astype(BF16), ffn2_w_down.astype(BF16)
    w_out_bf = w_out.astype(BF16)
    for i in range(depth):
        h = _ffn(h, row(ffn1_norm[i]), wgu1, wdn1, i)
        p_main, p_lora = _in_proj(h, row(mix_norm[i]), _pack_w_in(w_in[i], width, heads),
                                  scaled_blocks=width // PROJ_TN, scale=FOX_HEAD ** -0.5 * LOG2E)
        pm3 = p_main.reshape(b, lp, 6 * width)
        pl3 = p_lora.reshape(b, lp, LORA_W)
        prm = dict(
            mu_rkv=row(rwkv_mu[i, :3 * width]), mu_lora=_pack_mu_lora(rwkv_mu[i], width),
            w0=row(rwkv_w0[i]), w_up=_pad_rows(rwkv_w_up[i], LANES).astype(BF16),
            a0=row(rwkv_a0[i]), a_up=_pad_rows(rwkv_a_up[i], LANES).astype(BF16),
            g_up=_pad_rows(rwkv_g_up[i], 2 * LANES).astype(BF16),
            k_k=row(rwkv_k_k[i]), k_a=row(rwkv_k_a[i]), r_k=row(rwkv_r_k[i].reshape(-1)),
            lnx_w=row(rwkv_lnx_w[i]), lnx_b=row(rwkv_lnx_b[i]))
        y_rwkv = _rwkv(pm3, pl3, prm, width)
        bf_row = jnp.zeros((1, LANES), F32).at[0, f_lane0:f_lane0 + heads].set(fox_b_f[i])
        cp = _gate_cumsum(pl3, bf_row, f_blk, heads, f_lane0)
        vt3 = jnp.transpose(pm3[:, :, 2 * width:3 * width], (0, 2, 1))
        y_fox = _attention(pm3, vt3, cp, width)
        h = _out_proj(h, y_rwkv.reshape(t, width), y_fox.reshape(t, width), row(fox_out_norm[i]),
                      w_out_bf, i)
        h = _ffn(h, row(ffn2_norm[i]), wgu2, wdn2, i)
    assert seq % NORM_TM == 0 and NORM_TM % N_META == 0 and N_META % 8 == 0
    return _final_norm(h.reshape(b, lp, d), row(final_norm), seq)
```

```python
import functools
import math

import jax
import jax.numpy as jnp
from jax import lax
from jax.experimental import pallas as pl
from jax.experimental.pallas import tpu as pltpu

F32 = jnp.float32
BF16 = jnp.bfloat16

N_META = 16
RWKV_HEAD = 64
FOX_HEAD = 128
W_LORA = 64
A_LORA = 64
G_LORA = 160
NORM_EPS = 1e-6
LNX_EPS = 64e-5
L2_EPS = 1e-12
NEG_INF = -1e30
LOG2E = math.log2(math.e)

LANES = 128
VMEM_LIMIT_BYTES = 56 * 1024 * 1024

CHUNK = 64
RWKV_ROWS = 192
PAIR = 2 * RWKV_HEAD
TIME_ALIGN = 384
ATT_T = 384
GATE_TB = 384
FFN_TM = 1024
FFN_TF = 512
NORM_TM = 512
PROJ_TM = 1408
PROJ_TN = 512
OUT_TM = 512
LORA_W = 512
F_LANE = 448

_NT = (((1,), (1,)), ((), ()))


def _rms_scale(x, gain):
    ms = jnp.mean(x * x, axis=-1, keepdims=True)
    return x * lax.rsqrt(ms + NORM_EPS) * gain


def _dot(a, b):
    return jnp.dot(a.astype(BF16), b.astype(BF16), preferred_element_type=F32)


def _dot_nt(a, b):
    return lax.dot_general(a.astype(BF16), b.astype(BF16), _NT, preferred_element_type=F32)


def _split3(x):
    hi = x.astype(BF16)
    r1 = x - hi.astype(F32)
    mid = r1.astype(BF16)
    lo = (r1 - mid.astype(F32)).astype(BF16)
    return hi, mid, lo


def _dot_exact_lhs(m_bf16, x):
    hi, mid, lo = _split3(x)
    acc = jnp.dot(m_bf16, lo, preferred_element_type=F32)
    acc = acc + jnp.dot(m_bf16, mid, preferred_element_type=F32)
    return acc + jnp.dot(m_bf16, hi, preferred_element_type=F32)


def _params(*sem):
    return pltpu.CompilerParams(dimension_semantics=sem, vmem_limit_bytes=VMEM_LIMIT_BYTES)


def _ffn_body(h_ref, g_ref, wg_ref, wu_ref, wd_ref, o_ref, u_ref):
    f = pl.program_id(1)

    @pl.when(f == 0)
    def _():
        x = h_ref[...]
        u_ref[...] = _rms_scale(x, g_ref[...]).astype(BF16)
        o_ref[...] = x

    u = u_ref[...]
    gate = jnp.dot(u, wg_ref[0], preferred_element_type=F32)
    up = jnp.dot(u, wu_ref[0], preferred_element_type=F32)
    act = (0.5 * gate) * jax.nn.sigmoid(gate) * up
    o_ref[...] += jnp.dot(act.astype(BF16), wd_ref[0], preferred_element_type=F32)


def _ffn(h, gain, w_gu, w_down, layer):
    t, d = h.shape
    ff = w_down.shape[1]
    nf = ff // FFN_TF
    return pl.pallas_call(
        _ffn_body,
        grid=(t // FFN_TM, nf),
        in_specs=[
            pl.BlockSpec((FFN_TM, d), lambda i, f: (i, 0)),
            pl.BlockSpec((1, d), lambda i, f: (0, 0)),
            pl.BlockSpec((1, d, FFN_TF), lambda i, f: (layer, 0, f)),
            pl.BlockSpec((1, d, FFN_TF), lambda i, f: (layer, 0, f + nf)),
            pl.BlockSpec((1, FFN_TF, d), lambda i, f: (layer, f, 0)),
        ],
        out_specs=pl.BlockSpec((FFN_TM, d), lambda i, f: (i, 0)),
        out_shape=jax.ShapeDtypeStruct((t, d), F32),
        scratch_shapes=[pltpu.VMEM((FFN_TM, d), BF16)],
        compiler_params=_params("parallel", "arbitrary"),
    )(h, gain, w_gu, w_gu, w_down)


def _proj_body(h_ref, g_ref, w_ref, o_ref, ol_ref, u_ref, *, n_main, scaled_blocks, scale):
    j = pl.program_id(1)

    @pl.when(j == 0)
    def _():
        u_ref[...] = _rms_scale(h_ref[...], g_ref[...]).astype(BF16)

    acc = jnp.dot(u_ref[...], w_ref[...], preferred_element_type=F32)

    @pl.when(j < n_main)
    def _():
        o_ref[...] = (acc * jnp.where(j < scaled_blocks, scale, 1.0)).astype(o_ref.dtype)

    @pl.when(j == n_main)
    def _():
        ol_ref[...] = acc


def _in_proj(h, gain, w, scaled_blocks, scale):
    t, d = h.shape
    n = w.shape[1]
    n_main = n // PROJ_TN - 1
    return pl.pallas_call(
        functools.partial(_proj_body, n_main=n_main, scaled_blocks=scaled_blocks, scale=scale),
        grid=(t // PROJ_TM, n_main + 1),
        in_specs=[
            pl.BlockSpec((PROJ_TM, d), lambda i, j: (i, 0)),
            pl.BlockSpec((1, d), lambda i, j: (0, 0)),
            pl.BlockSpec((d, PROJ_TN), lambda i, j: (0, j)),
        ],
        out_specs=[
            pl.BlockSpec((PROJ_TM, PROJ_TN), lambda i, j: (i, jnp.minimum(j, n_main - 1))),
            pl.BlockSpec((PROJ_TM, PROJ_TN), lambda i, j: (i, 0)),
        ],
        out_shape=[jax.ShapeDtypeStruct((t, n_main * PROJ_TN), BF16),
                   jax.ShapeDtypeStruct((t, PROJ_TN), F32)],
        scratch_shapes=[pltpu.VMEM((PROJ_TM, d), BF16)],
        compiler_params=_params("parallel", "arbitrary"),
    )(h, gain, w)


def _rwkv_body(rkv_ref, lora_ref, mu_rkv_ref, mu_lora_ref, w0_ref, wup_ref, a0_ref, aup_ref,
               gup_ref, kk_ref, ka_ref, rk_ref, lnw_ref, lnb_ref, o_ref,
               carry_rkv, carry_lora, st_ref, *, width):
    rows = RWKV_ROWS
    n_ch = rows // CHUNK
    c = pl.program_id(1)

    @pl.when(c == 0)
    def _():
        carry_rkv[...] = jnp.zeros_like(carry_rkv)
        carry_lora[...] = jnp.zeros_like(carry_lora)
        st_ref[...] = jnp.zeros_like(st_ref)

    row = lax.broadcasted_iota(jnp.int32, (rows, 1), 0)

    def shift(p, carry_ref, mu):
        prev = jnp.where(row == 0, carry_ref[...], pltpu.roll(p, 1, 0))
        carry_ref[...] = p[rows - 1:rows, :]
        return p + (prev - p) * mu

    pf = shift(rkv_ref[0].astype(F32), carry_rkv, mu_rkv_ref[...])
    lf = shift(lora_ref[0], carry_lora, mu_lora_ref[...])
    r = pf[:, :width]
    k = pf[:, width:2 * width]
    v = pf[:, 2 * width:]
    wd = lf[:, 0:LANES]
    ad = lf[:, LANES:2 * LANES]
    gd = lf[:, 2 * LANES:4 * LANES]

    z = w0_ref[...] + _dot(jnp.tanh(wd), wup_ref[...])
    nz = -z
    softplus = jnp.maximum(nz, 0.0) + jnp.log1p(jnp.exp(-jnp.abs(nz)))
    logw = -jnp.exp(-softplus - 0.5)
    a = jax.nn.sigmoid(a0_ref[...] + _dot(ad, aup_ref[...]))
    g = _dot(jax.nn.sigmoid(gd), gup_ref[...])

    ci = lax.broadcasted_iota(jnp.int32, (rows, rows), 0)
    cj = lax.broadcasted_iota(jnp.int32, (rows, rows), 1)
    ltri = ((cj <= ci) & ((ci // CHUNK) == (cj // CHUNK))).astype(BF16)
    cs = _dot_exact_lhs(ltri, logw)

    ii = lax.broadcasted_iota(jnp.int32, (PAIR, PAIR), 0)
    jj = lax.broadcasted_iota(jnp.int32, (PAIR, PAIR), 1)
    same_head = (ii // RWKV_HEAD) == (jj // RWKV_HEAD)
    mask_strict = same_head & (jj < ii)
    mask_incl = same_head & (jj <= ii)
    eye = ii == jj
    seg_ones = same_head.astype(BF16)
    lane = lax.broadcasted_iota(jnp.int32, (CHUNK, PAIR), 1)
    head0 = lane < RWKV_HEAD

    def stack(x):
        return jnp.concatenate([jnp.where(head0, x, 0.0), jnp.where(head0, 0.0, x)], axis=0)

    def both(x):
        return jnp.concatenate([x, x], axis=0)

    pairs = range(width // PAIR)
    sls = [slice(hp * PAIR, (hp + 1) * PAIR) for hp in pairs]
    chunks = [slice(ch * CHUNK, (ch + 1) * CHUNK) for ch in range(n_ch)]
    units = [(ch, p) for ch in range(n_ch) for p in pairs]

    kk = [k[:, sl] * kk_ref[:, sl] for sl in sls]
    ss = [_dot(x * x, seg_ones) for x in kk]
    kk = [x / jnp.maximum(jnp.sqrt(s), L2_EPS) for x, s in zip(kk, ss)]
    k2 = [k[:, sl] * (1.0 + (a[:, sl] - 1.0) * ka_ref[:, sl]) for sl in sls]
    bv = [x * a[:, sl] for x, sl in zip(kk, sls)]
    e_neg = [jnp.exp(-cs[:, sl]) for sl in sls]
    at_ = [-kk[p] * jnp.exp(cs[:, sls[p]] - logw[:, sls[p]]) for p in pairs]
    rt_ = [r[:, sl] * jnp.exp(cs[:, sl]) for sl in sls]
    bt_ = [bv[p] * e_neg[p] for p in pairs]
    kt_ = [k2[p] * e_neg[p] for p in pairs]

    last = [cs[chunks[ch]][CHUNK - 1:CHUNK, sls[p]] for ch, p in units]
    e_end = [jnp.exp(last[u] - cs[chunks[ch], sls[p]]) for u, (ch, p) in enumerate(units)]
    xa = [stack(at_[p][chunks[ch]]) for ch, p in units]
    xr = [stack(rt_[p][chunks[ch]]) for ch, p in units]
    xbh_t = [stack(bv[p][chunks[ch]] * e_end[u]).T for u, (ch, p) in enumerate(units)]
    xkh_t = [stack(k2[p][chunks[ch]] * e_end[u]).T for u, (ch, p) in enumerate(units)]
    vst = [stack(v[chunks[ch], sls[p]]) for ch, p in units]

    big = [_dot_nt(jnp.concatenate([xa[u], xr[u]], axis=0),
                   jnp.concatenate([both(bt_[p][chunks[ch]]), both(kt_[p][chunks[ch]])], axis=0))
           for u, (ch, p) in enumerate(units)]
    big = [x.astype(BF16) for x in big]
    m_strict, m_incl = mask_strict.astype(BF16), mask_incl.astype(BF16)
    a_ab = [x[:PAIR, :PAIR] for x in big]
    a_ak = [x[:PAIR, PAIR:] * m_strict for x in big]
    a_rb = [x[PAIR:, :PAIR] * m_incl for x in big]
    a_rk = [x[PAIR:, PAIR:] * m_incl for x in big]

    lvl1 = (((ii ^ jj) == 1) & (jj < ii)).astype(BF16)
    eye_bf = eye.astype(BF16)
    tinv = [eye_bf + x * lvl1 for x in a_ab]
    s = 2
    while s < RWKV_HEAD:
        e_mask = (((ii // (2 * s)) == (jj // (2 * s))) & (((ii // s) % 2) == 1)
                  & (((jj // s) % 2) == 0)).astype(BF16)
        half = [_dot(t_, x * e_mask).astype(BF16) for t_, x in zip(tinv, a_ab)]
        tinv = [t_ + _dot(h_, t_).astype(BF16) for t_, h_ in zip(tinv, half)]
        s *= 2

    n_u = range(len(units))
    w_av = [_dot(a_ak[u], vst[u]) for u in n_u]
    x = [_dot(tinv[u], jnp.concatenate([xa[u], w_av[u]], axis=1)) for u in n_u]
    mg = [_dot(xbh_t[u], x[u]) for u in n_u]
    kv = [_dot(xkh_t[u], vst[u]) for u in n_u]
    ry = [_dot(a_rb[u], x[u]) for u in n_u]
    rkv_ = [_dot(a_rk[u], vst[u]) for u in n_u]
    rh = [xr[u] + ry[u][:, :PAIR] for u in n_u]
    y0 = [ry[u][:, PAIR:] + rkv_[u] for u in n_u]
    mm = [mg[u][:, :PAIR] + jnp.where(eye, jnp.exp(last[u]), 0.0) for u in n_u]
    gg = [mg[u][:, PAIR:] + kv[u] for u in n_u]

    st = [st_ref[p] for p in pairs]
    yst = []
    for ch in range(n_ch):
        base = ch * len(pairs)
        yst.append([_dot(rh[base + p], st[p]) + y0[base + p] for p in pairs])
        st = [_dot(mm[base + p], st[p]) + gg[base + p] for p in pairs]
    for p in pairs:
        st_ref[p] = st[p]
    y = [jnp.concatenate([yst[ch][p][:CHUNK] + yst[ch][p][CHUNK:] for ch in range(n_ch)], axis=0)
         for p in pairs]

    inv_n = 1.0 / RWKV_HEAD
    mean = [_dot(x_, seg_ones) * inv_n for x_ in y]
    dlt = [x_ - m_ for x_, m_ in zip(y, mean)]
    var = [_dot(x_ * x_, seg_ones) * inv_n for x_ in dlt]
    bonus = [_dot(r[:, sl] * k2[p] * rk_ref[:, sl], seg_ones) * v[:, sl] for p, sl in zip(pairs, sls)]
    for p, sl in zip(pairs, sls):
        yn = dlt[p] * lax.rsqrt(var[p] + LNX_EPS) * lnw_ref[:, sl] + lnb_ref[:, sl]
        o_ref[0, :, sl] = ((yn + bonus[p]) * g[:, sl]).astype(o_ref.dtype)


def _rwkv(pm3, pl3, prm, width):
    b, lp, _ = pm3.shape
    n_rkv = 3 * width
    vec = lambda n: pl.BlockSpec((1, n), lambda i, c: (0, 0))
    mat = lambda k: pl.BlockSpec((k, width), lambda i, c: (0, 0))
    return pl.pallas_call(
        functools.partial(_rwkv_body, width=width),
        grid=(b, lp // RWKV_ROWS),
        in_specs=[
            pl.BlockSpec((1, RWKV_ROWS, n_rkv), lambda i, c: (i, c, 1)),
            pl.BlockSpec((1, RWKV_ROWS, LORA_W), lambda i, c: (i, c, 0)),
            vec(n_rkv), vec(LORA_W), vec(width), mat(LANES), vec(width), mat(LANES),
            mat(2 * LANES), vec(width), vec(width), vec(width), vec(width), vec(width),
        ],
        out_specs=pl.BlockSpec((1, RWKV_ROWS, width), lambda i, c: (i, c, 0)),
        out_shape=jax.ShapeDtypeStruct((b, lp, width), BF16),
        scratch_shapes=[
            pltpu.VMEM((1, n_rkv), F32),
            pltpu.VMEM((1, LORA_W), F32),
            pltpu.VMEM((width // PAIR, PAIR, PAIR), F32),
        ],
        compiler_params=_params("parallel", "arbitrary"),
    )(pm3, pl3, prm["mu_rkv"], prm["mu_lora"], prm["w0"], prm["w_up"], prm["a0"], prm["a_up"],
      prm["g_up"], prm["k_k"], prm["k_a"], prm["r_k"], prm["lnx_w"], prm["lnx_b"])


def _gate_body(x_ref, bf_ref, o_ref, carry_ref, *, heads, lane0):
    @pl.when(pl.program_id(1) == 0)
    def _():
        carry_ref[...] = jnp.zeros_like(carry_ref)

    zz = x_ref[0] + bf_ref[...]
    log_f = jnp.minimum(zz, 0.0) - jnp.log1p(jnp.exp(-jnp.abs(zz)))
    ci = lax.broadcasted_iota(jnp.int32, (GATE_TB, GATE_TB), 0)
    cj = lax.broadcasted_iota(jnp.int32, (GATE_TB, GATE_TB), 1)
    csum = _dot_exact_lhs((cj <= ci).astype(BF16), log_f) + carry_ref[...]
    carry_ref[...] = csum[GATE_TB - 1:GATE_TB, :]
    pieces = _split3(csum * LOG2E)
    src = lax.broadcasted_iota(jnp.int32, (LANES, LANES), 0)
    dst = lax.broadcasted_iota(jnp.int32, (LANES, LANES), 1)
    for g in range(heads):
        acc = jnp.zeros((GATE_TB, LANES), F32)
        for i, piece in enumerate(pieces):
            move = ((src == lane0 + g) & (dst == i)).astype(BF16)
            acc = acc + jnp.dot(piece, move, preferred_element_type=F32)
        o_ref[0, g] = acc.astype(BF16)


def _gate_cumsum(p3, bf_row, lane_blk, heads, lane0):
    b, lp, _ = p3.shape
    return pl.pallas_call(
        functools.partial(_gate_body, heads=heads, lane0=lane0),
        grid=(b, lp // GATE_TB),
        in_specs=[
            pl.BlockSpec((1, GATE_TB, LANES), lambda i, c: (i, c, lane_blk)),
            pl.BlockSpec((1, LANES), lambda i, c: (0, 0)),
        ],
        out_specs=pl.BlockSpec((1, heads, GATE_TB, LANES), lambda i, c: (i, 0, c, 0)),
        out_shape=jax.ShapeDtypeStruct((b, heads, lp, LANES), BF16),
        scratch_shapes=[pltpu.VMEM((1, LANES), F32)],
        compiler_params=_params("parallel", "arbitrary"),
    )(p3, bf_row)


def _attn_body(q_ref, k_ref, vt_ref, cp_ref, o_ref, m_ref, l_ref, acc_ref, *, heads):
    qi = pl.program_id(1)
    m_ref[...] = jnp.full_like(m_ref, NEG_INF)
    l_ref[...] = jnp.zeros_like(l_ref)
    acc_ref[...] = jnp.zeros_like(acc_ref)
    hs = [slice(g * FOX_HEAD, (g + 1) * FOX_HEAD) for g in range(heads)]
    gs = range(heads)
    neg_ones = -(lax.broadcasted_iota(jnp.int32, (ATT_T, LANES), 1) < 3).astype(BF16)
    q_aug = [jnp.concatenate([q_ref[0, :, hs[g]], neg_ones], axis=1) for g in gs]

    def block(j, masked):
        off = pl.multiple_of(j * ATT_T, ATT_T)
        s = [lax.dot_general(
            jnp.concatenate([k_ref[0, pl.ds(off, ATT_T), hs[g]], cp_ref[0, g, pl.ds(off, ATT_T), :]], axis=1),
            q_aug[g], _NT, preferred_element_type=F32) for g in gs]
        if masked:
            kpos = lax.broadcasted_iota(jnp.int32, (ATT_T, ATT_T), 0)
            qpos = lax.broadcasted_iota(jnp.int32, (ATT_T, ATT_T), 1)
            s = [jnp.where(kpos <= qpos, x, NEG_INF) for x in s]
        m_prev = [m_ref[g] for g in gs]
        m_new = [jnp.maximum(m_prev[g], jnp.max(s[g], axis=0, keepdims=True)) for g in gs]
        p = [jnp.exp2(s[g] - m_new[g]) for g in gs]
        alpha = [jnp.exp2(m_prev[g] - m_new[g]) for g in gs]
        pv = [jnp.dot(vt_ref[0, hs[g], pl.ds(off, ATT_T)], p[g].astype(BF16),
                      preferred_element_type=F32) for g in gs]
        for g in gs:
            l_ref[g] = alpha[g] * l_ref[g] + jnp.sum(p[g], axis=0, keepdims=True)
            acc_ref[hs[g], :] = alpha[g] * acc_ref[hs[g], :] + pv[g]
            m_ref[g] = m_new[g]

    def body(j, carry):
        block(j, False)
        return carry

    lax.fori_loop(0, qi, body, 0)
    block(qi, True)
    for g in gs:
        o_ref[0, :, hs[g]] = (acc_ref[hs[g], :] / l_ref[g]).T


def _attention(qkv3, vt3, cp, width):
    b, lp, _ = qkv3.shape
    heads = width // FOX_HEAD
    resident = pl.Buffered(1)
    return pl.pallas_call(
        functools.partial(_attn_body, heads=heads),
        grid=(b, lp // ATT_T),
        in_specs=[
            pl.BlockSpec((1, ATT_T, width), lambda i, q: (i, q, 0)),
            pl.BlockSpec((1, lp, width), lambda i, q: (i, 0, 1), pipeline_mode=resident),
            pl.BlockSpec((1, width, lp), lambda i, q: (i, 0, 0), pipeline_mode=resident),
            pl.BlockSpec((1, heads, lp, LANES), lambda i, q: (i, 0, 0, 0), pipeline_mode=resident),
        ],
        out_specs=pl.BlockSpec((1, ATT_T, width), lambda i, q: (i, q, 0)),
        out_shape=jax.ShapeDtypeStruct((b, lp, width), F32),
        scratch_shapes=[
            pltpu.VMEM((heads, 1, ATT_T), F32),
            pltpu.VMEM((heads, 1, ATT_T), F32),
            pltpu.VMEM((width, ATT_T), F32),
        ],
        compiler_params=_params("parallel", "arbitrary"),
    )(qkv3, qkv3, vt3, cp)


def _out_body(h_ref, yr_ref, yf_ref, gn_ref, w1_ref, w2_ref, o_ref):
    yf = _rms_scale(yf_ref[...], gn_ref[...])
    acc = jnp.dot(yr_ref[...], w1_ref[0], preferred_element_type=F32)
    acc = acc + jnp.dot(yf.astype(BF16), w2_ref[0], preferred_element_type=F32)
    o_ref[...] = h_ref[...] + acc


def _out_proj(h, y_rwkv, y_fox, fox_gain, w_out, layer):
    t, d = h.shape
    w1 = y_rwkv.shape[1]
    w2 = y_fox.shape[1]
    return pl.pallas_call(
        _out_body,
        grid=(t // OUT_TM,),
        in_specs=[
            pl.BlockSpec((OUT_TM, d), lambda i: (i, 0)),
            pl.BlockSpec((OUT_TM, w1), lambda i: (i, 0)),
            pl.BlockSpec((OUT_TM, w2), lambda i: (i, 0)),
            pl.BlockSpec((1, w2), lambda i: (0, 0)),
            pl.BlockSpec((1, w1, d), lambda i: (layer, 0, 0)),
            pl.BlockSpec((1, w2, d), lambda i: (layer, 1, 0)),
        ],
        out_specs=pl.BlockSpec((OUT_TM, d), lambda i: (i, 0)),
        out_shape=jax.ShapeDtypeStruct((t, d), F32),
        compiler_params=_params("parallel"),
    )(h, y_rwkv, y_fox, fox_gain, w_out, w_out)


def _norm_body(a_ref, b_ref, g_ref, o_ref):
    x = jnp.concatenate([a_ref[0, N_META:, :], b_ref[0]], axis=0)
    o_ref[0] = _rms_scale(x, g_ref[...])


def _final_norm(h3, gain, seq):
    b, _, d = h3.shape
    per = NORM_TM // N_META
    return pl.pallas_call(
        _norm_body,
        grid=(b, seq // NORM_TM),
        in_specs=[
            pl.BlockSpec((1, NORM_TM, d), lambda i, r: (i, r, 0)),
            pl.BlockSpec((1, N_META, d), lambda i, r: (i, (r + 1) * per, 0)),
            pl.BlockSpec((1, d), lambda i, r: (0, 0)),
        ],
        out_specs=pl.BlockSpec((1, NORM_TM, d), lambda i, r: (i, r, 0)),
        out_shape=jax.ShapeDtypeStruct((b, seq, d), F32),
        compiler_params=_params("parallel", "parallel"),
    )(h3, h3, gain)


def _pad_rows(w, rows):
    return jnp.pad(w, ((0, rows - w.shape[0]), (0, 0)))


def _pack_w_in(w_in, width, heads):
    d = w_in.shape[0]
    c3 = 3 * width
    c4, c5, c6 = c3 + W_LORA, c3 + W_LORA + A_LORA, c3 + W_LORA + A_LORA + G_LORA
    lora = jnp.zeros((d, LORA_W), w_in.dtype)
    lora = lora.at[:, 0:W_LORA].set(w_in[:, c3:c4])
    lora = lora.at[:, LANES:LANES + A_LORA].set(w_in[:, c4:c5])
    lora = lora.at[:, 2 * LANES:2 * LANES + G_LORA].set(w_in[:, c5:c6])
    lora = lora.at[:, F_LANE:F_LANE + heads].set(w_in[:, c6 + c3:c6 + c3 + heads])
    return jnp.concatenate([w_in[:, c6:c6 + c3], w_in[:, :c3], lora], axis=1).astype(BF16)


def _pack_mu_lora(mu, width):
    c3 = 3 * width
    c4, c5, c6 = c3 + W_LORA, c3 + W_LORA + A_LORA, c3 + W_LORA + A_LORA + G_LORA
    out = jnp.zeros((LORA_W,), mu.dtype)
    out = out.at[0:W_LORA].set(mu[c3:c4])
    out = out.at[LANES:LANES + A_LORA].set(mu[c4:c5])
    out = out.at[2 * LANES:2 * LANES + G_LORA].set(mu[c5:c6])
    return out[None, :]


def kernel(x, meta_tokens, ffn1_norm, ffn1_w_gu, ffn1_w_down, mix_norm, w_in, rwkv_mu, rwkv_w0,
           rwkv_w_up, rwkv_a0, rwkv_a_up, rwkv_g_up, rwkv_k_k, rwkv_k_a, rwkv_r_k, rwkv_lnx_w,
           rwkv_lnx_b, fox_b_f, fox_out_norm, w_out, ffn2_norm, ffn2_w_gu, ffn2_w_down, final_norm):
    b, seq, d = x.shape
    depth = w_in.shape[0]
    width = rwkv_w0.shape[1]
    heads = fox_b_f.shape[1]
    assert rwkv_g_up.shape[1] == G_LORA and G_LORA <= 2 * LANES
    assert F_LANE >= 2 * LANES + G_LORA and F_LANE + heads <= LORA_W and F_LANE % LANES + heads <= LANES
    l = N_META + seq
    lp = -(-l // TIME_ALIGN) * TIME_ALIGN
    t = b * lp
    assert t % PROJ_TM == 0 and t % FFN_TM == 0 and lp % RWKV_ROWS == 0 and RWKV_ROWS % CHUNK == 0
    assert width % PROJ_TN == 0 and LORA_W == PROJ_TN

    meta = jnp.broadcast_to(meta_tokens.astype(x.dtype)[None], (b, N_META, d))
    h = jnp.concatenate([meta, x, jnp.zeros((b, lp - l, d), x.dtype)], axis=1).reshape(t, d)

    row = lambda vct: vct.astype(F32)[None, :]
    f_blk = F_LANE // LANES
    f_lane0 = F_LANE % LANES
    wgu1, wdn1 = ffn1_w_gu.astype(BF16), ffn1_w_down.astype(BF16)
    wgu2, wdn2 = ffn2_w_gu.astype(BF16), ffn2_w_down.astype(BF16)
    w_out_bf = w_out.astype(BF16)
    for i in range(depth):
        h = _ffn(h, row(ffn1_norm[i]), wgu1, wdn1, i)
        p_main, p_lora = _in_proj(h, row(mix_norm[i]), _pack_w_in(w_in[i], width, heads),
                                  scaled_blocks=width // PROJ_TN, scale=FOX_HEAD ** -0.5 * LOG2E)
        pm3 = p_main.reshape(b, lp, 6 * width)
        pl3 = p_lora.reshape(b, lp, LORA_W)
        prm = dict(
            mu_rkv=row(rwkv_mu[i, :3 * width]), mu_lora=_pack_mu_lora(rwkv_mu[i], width),
            w0=row(rwkv_w0[i]), w_up=_pad_rows(rwkv_w_up[i], LANES).astype(BF16),
            a0=row(rwkv_a0[i]), a_up=_pad_rows(rwkv_a_up[i], LANES).astype(BF16),
            g_up=_pad_rows(rwkv_g_up[i], 2 * LANES).astype(BF16),
            k_k=row(rwkv_k_k[i]), k_a=row(rwkv_k_a[i]), r_k=row(rwkv_r_k[i].reshape(-1)),
            lnx_w=row(rwkv_lnx_w[i]), lnx_b=row(rwkv_lnx_b[i]))
        y_rwkv = _rwkv(pm3, pl3, prm, width)
        bf_row = jnp.zeros((1, LANES), F32).at[0, f_lane0:f_lane0 + heads].set(fox_b_f[i])
        cp = _gate_cumsum(pl3, bf_row, f_blk, heads, f_lane0)
        vt3 = jnp.transpose(pm3[:, :, 2 * width:3 * width], (0, 2, 1))
        y_fox = _attention(pm3, vt3, cp, width)
        h = _out_proj(h, y_rwkv.reshape(t, width), y_fox.reshape(t, width), row(fox_out_norm[i]),
                      w_out_bf, i)
        h = _ffn(h, row(ffn2_norm[i]), wgu2, wdn2, i)
    assert seq % NORM_TM == 0 and NORM_TM % N_META == 0 and N_META % 8 == 0
    return _final_norm(h.reshape(b, lp, d), row(final_norm), seq)
```

```python
import functools
import math

import jax
import jax.numpy as jnp
from jax import lax
from jax.experimental import pallas as pl
from jax.experimental.pallas import tpu as pltpu

F32 = jnp.float32
BF16 = jnp.bfloat16

N_META = 16
RWKV_HEAD = 64
FOX_HEAD = 128
W_LORA = 64
A_LORA = 64
G_LORA = 160
NORM_EPS = 1e-6
LNX_EPS = 64e-5
L2_EPS = 1e-12
NEG_INF = -1e30
LOG2E = math.log2(math.e)
BIAS_PIECES = 3

LANES = 128
VMEM_LIMIT_BYTES = 56 * 1024 * 1024

CHUNK = 64
RWKV_ROWS = 192
PAIR = 2 * RWKV_HEAD
TIME_ALIGN = 384
ATT_T = 384
GATE_TB = 384
FFN_TM = 1024
FFN_TF = 512
NORM_TM = 512
PROJ_TM = 1408
PROJ_TN = 512
OUT_TM = 512
LORA_W = 512
F_LANE = 448

_NT = (((1,), (1,)), ((), ()))


def _rms_scale(x, gain):
    ms = jnp.mean(x * x, axis=-1, keepdims=True)
    return x * lax.rsqrt(ms + NORM_EPS) * gain


def _dot(a, b):
    return jnp.dot(a.astype(BF16), b.astype(BF16), preferred_element_type=F32)


def _dot_nt(a, b):
    return lax.dot_general(a.astype(BF16), b.astype(BF16), _NT, preferred_element_type=F32)


def _split3(x):
    hi = x.astype(BF16)
    r1 = x - hi.astype(F32)
    mid = r1.astype(BF16)
    lo = (r1 - mid.astype(F32)).astype(BF16)
    return hi, mid, lo


def _dot_exact_lhs(m_bf16, x):
    hi, mid, lo = _split3(x)
    acc = jnp.dot(m_bf16, lo, preferred_element_type=F32)
    acc = acc + jnp.dot(m_bf16, mid, preferred_element_type=F32)
    return acc + jnp.dot(m_bf16, hi, preferred_element_type=F32)


def _params(*sem):
    return pltpu.CompilerParams(dimension_semantics=sem, vmem_limit_bytes=VMEM_LIMIT_BYTES)


def _ffn_body(h_ref, g_ref, wg_ref, wu_ref, wd_ref, o_ref, u_ref):
    f = pl.program_id(1)

    @pl.when(f == 0)
    def _():
        x = h_ref[...]
        u_ref[...] = _rms_scale(x, g_ref[...]).astype(BF16)
        o_ref[...] = x

    u = u_ref[...]
    gate = jnp.dot(u, wg_ref[0], preferred_element_type=F32)
    up = jnp.dot(u, wu_ref[0], preferred_element_type=F32)
    act = (0.5 * gate) * jax.nn.sigmoid(gate) * up
    o_ref[...] += jnp.dot(act.astype(BF16), wd_ref[0], preferred_element_type=F32)


def _ffn(h, gain, w_gu, w_down, layer):
    t, d = h.shape
    ff = w_down.shape[1]
    nf = ff // FFN_TF
    return pl.pallas_call(
        _ffn_body,
        grid=(t // FFN_TM, nf),
        in_specs=[
            pl.BlockSpec((FFN_TM, d), lambda i, f: (i, 0)),
            pl.BlockSpec((1, d), lambda i, f: (0, 0)),
            pl.BlockSpec((1, d, FFN_TF), lambda i, f: (layer, 0, f)),
            pl.BlockSpec((1, d, FFN_TF), lambda i, f: (layer, 0, f + nf)),
            pl.BlockSpec((1, FFN_TF, d), lambda i, f: (layer, f, 0)),
        ],
        out_specs=pl.BlockSpec((FFN_TM, d), lambda i, f: (i, 0)),
        out_shape=jax.ShapeDtypeStruct((t, d), F32),
        scratch_shapes=[pltpu.VMEM((FFN_TM, d), BF16)],
        compiler_params=_params("parallel", "arbitrary"),
    )(h, gain, w_gu, w_gu, w_down)


def _proj_body(h_ref, g_ref, w_ref, o_ref, ol_ref, u_ref, *, n_main, scaled_blocks, scale):
    j = pl.program_id(1)

    @pl.when(j == 0)
    def _():
        u_ref[...] = _rms_scale(h_ref[...], g_ref[...]).astype(BF16)

    acc = jnp.dot(u_ref[...], w_ref[...], preferred_element_type=F32)

    @pl.when(j < n_main)
    def _():
        o_ref[...] = (acc * jnp.where(j < scaled_blocks, scale, 1.0)).astype(o_ref.dtype)

    @pl.when(j == n_main)
    def _():
        ol_ref[...] = acc


def _in_proj(h, gain, w, scaled_blocks, scale):
    t, d = h.shape
    n = w.shape[1]
    n_main = n // PROJ_TN - 1
    return pl.pallas_call(
        functools.partial(_proj_body, n_main=n_main, scaled_blocks=scaled_blocks, scale=scale),
        grid=(t // PROJ_TM, n_main + 1),
        in_specs=[
            pl.BlockSpec((PROJ_TM, d), lambda i, j: (i, 0)),
            pl.BlockSpec((1, d), lambda i, j: (0, 0)),
            pl.BlockSpec((d, PROJ_TN), lambda i, j: (0, j)),
        ],
        out_specs=[
            pl.BlockSpec((PROJ_TM, PROJ_TN), lambda i, j: (i, jnp.minimum(j, n_main - 1))),
            pl.BlockSpec((PROJ_TM, PROJ_TN), lambda i, j: (i, 0)),
        ],
        out_shape=[jax.ShapeDtypeStruct((t, n_main * PROJ_TN), BF16),
                   jax.ShapeDtypeStruct((t, PROJ_TN), F32)],
        scratch_shapes=[pltpu.VMEM((PROJ_TM, d), BF16)],
        compiler_params=_params("parallel", "arbitrary"),
    )(h, gain, w)


def _rwkv_body(rkv_ref, lora_ref, mu_rkv_ref, mu_lora_ref, w0_ref, wup_ref, a0_ref, aup_ref,
               gup_ref, kk_ref, ka_ref, rk_ref, lnw_ref, lnb_ref, o_ref,
               carry_rkv, carry_lora, st_ref, *, width):
    rows = RWKV_ROWS
    n_ch = rows // CHUNK
    c = pl.program_id(1)

    @pl.when(c == 0)
    def _():
        carry_rkv[...] = jnp.zeros_like(carry_rkv)
        carry_lora[...] = jnp.zeros_like(carry_lora)
        st_ref[...] = jnp.zeros_like(st_ref)

    row = lax.broadcasted_iota(jnp.int32, (rows, 1), 0)

    def shift(p, carry_ref, mu):
        prev = jnp.where(row == 0, carry_ref[...], pltpu.roll(p, 1, 0))
        carry_ref[...] = p[rows - 1:rows, :]
        return p + (prev - p) * mu

    pf = shift(rkv_ref[0].astype(F32), carry_rkv, mu_rkv_ref[...])
    lf = shift(lora_ref[0], carry_lora, mu_lora_ref[...])
    r = pf[:, :width]
    k = pf[:, width:2 * width]
    v = pf[:, 2 * width:]
    wd = lf[:, 0:LANES]
    ad = lf[:, LANES:2 * LANES]
    gd = lf[:, 2 * LANES:4 * LANES]

    z = w0_ref[...] + _dot(jnp.tanh(wd), wup_ref[...])
    nz = -z
    softplus = jnp.maximum(nz, 0.0) + jnp.log1p(jnp.exp(-jnp.abs(nz)))
    logw = -jnp.exp(-softplus - 0.5)
    a = jax.nn.sigmoid(a0_ref[...] + _dot(ad, aup_ref[...]))
    g = _dot(jax.nn.sigmoid(gd), gup_ref[...])

    ci = lax.broadcasted_iota(jnp.int32, (rows, rows), 0)
    cj = lax.broadcasted_iota(jnp.int32, (rows, rows), 1)
    ltri = ((cj <= ci) & ((ci // CHUNK) == (cj // CHUNK))).astype(BF16)
    cs = _dot_exact_lhs(ltri, logw)

    ii = lax.broadcasted_iota(jnp.int32, (PAIR, PAIR), 0)
    jj = lax.broadcasted_iota(jnp.int32, (PAIR, PAIR), 1)
    same_head = (ii // RWKV_HEAD) == (jj // RWKV_HEAD)
    mask_strict = same_head & (jj < ii)
    mask_incl = same_head & (jj <= ii)
    eye = ii == jj
    seg_ones = same_head.astype(BF16)
    lane = lax.broadcasted_iota(jnp.int32, (CHUNK, PAIR), 1)
    head0 = lane < RWKV_HEAD

    def stack(x):
        return jnp.concatenate([jnp.where(head0, x, 0.0), jnp.where(head0, 0.0, x)], axis=0)

    def both(x):
        return jnp.concatenate([x, x], axis=0)

    pairs = range(width // PAIR)
    sls = [slice(hp * PAIR, (hp + 1) * PAIR) for hp in pairs]
    chunks = [slice(ch * CHUNK, (ch + 1) * CHUNK) for ch in range(n_ch)]
    units = [(ch, p) for ch in range(n_ch) for p in pairs]

    kk = [k[:, sl] * kk_ref[:, sl] for sl in sls]
    ss = [_dot(x * x, seg_ones) for x in kk]
    kk = [x / jnp.maximum(jnp.sqrt(s), L2_EPS) for x, s in zip(kk, ss)]
    k2 = [k[:, sl] * (1.0 + (a[:, sl] - 1.0) * ka_ref[:, sl]) for sl in sls]
    bv = [x * a[:, sl] for x, sl in zip(kk, sls)]
    e_neg = [jnp.exp(-cs[:, sl]) for sl in sls]
    at_ = [-kk[p] * jnp.exp(cs[:, sls[p]] - logw[:, sls[p]]) for p in pairs]
    rt_ = [r[:, sl] * jnp.exp(cs[:, sl]) for sl in sls]
    bt_ = [bv[p] * e_neg[p] for p in pairs]
    kt_ = [k2[p] * e_neg[p] for p in pairs]

    last = [cs[chunks[ch]][CHUNK - 1:CHUNK, sls[p]] for ch, p in units]
    e_end = [jnp.exp(last[u] - cs[chunks[ch], sls[p]]) for u, (ch, p) in enumerate(units)]
    xa = [stack(at_[p][chunks[ch]]) for ch, p in units]
    xr = [stack(rt_[p][chunks[ch]]) for ch, p in units]
    xbh_t = [stack(bv[p][chunks[ch]] * e_end[u]).T for u, (ch, p) in enumerate(units)]
    xkh_t = [stack(k2[p][chunks[ch]] * e_end[u]).T for u, (ch, p) in enumerate(units)]
    vst = [stack(v[chunks[ch], sls[p]]) for ch, p in units]

    big = [_dot_nt(jnp.concatenate([xa[u], xr[u]], axis=0),
                   jnp.concatenate([both(bt_[p][chunks[ch]]), both(kt_[p][chunks[ch]])], axis=0))
           for u, (ch, p) in enumerate(units)]
    big = [x.astype(BF16) for x in big]
    m_strict, m_incl = mask_strict.astype(BF16), mask_incl.astype(BF16)
    a_ab = [x[:PAIR, :PAIR] for x in big]
    a_ak = [x[:PAIR, PAIR:] * m_strict for x in big]
    a_rb = [x[PAIR:, :PAIR] * m_incl for x in big]
    a_rk = [x[PAIR:, PAIR:] * m_incl for x in big]

    lvl1 = (((ii ^ jj) == 1) & (jj < ii)).astype(BF16)
    eye_bf = eye.astype(BF16)
    tinv = [eye_bf + x * lvl1 for x in a_ab]
    s = 2
    while s < RWKV_HEAD:
        e_mask = (((ii // (2 * s)) == (jj // (2 * s))) & (((ii // s) % 2) == 1)
                  & (((jj // s) % 2) == 0)).astype(BF16)
        half = [_dot(t_, x * e_mask).astype(BF16) for t_, x in zip(tinv, a_ab)]
        tinv = [t_ + _dot(h_, t_).astype(BF16) for t_, h_ in zip(tinv, half)]
        s *= 2

    n_u = range(len(units))
    w_av = [_dot(a_ak[u], vst[u]) for u in n_u]
    x = [_dot(tinv[u], jnp.concatenate([xa[u], w_av[u]], axis=1)) for u in n_u]
    mg = [_dot(xbh_t[u], x[u]) for u in n_u]
    kv = [_dot(xkh_t[u], vst[u]) for u in n_u]
    ry = [_dot(a_rb[u], x[u]) for u in n_u]
    rkv_ = [_dot(a_rk[u], vst[u]) for u in n_u]
    rh = [xr[u] + ry[u][:, :PAIR] for u in n_u]
    y0 = [ry[u][:, PAIR:] + rkv_[u] for u in n_u]
    mm = [mg[u][:, :PAIR] + jnp.where(eye, jnp.exp(last[u]), 0.0) for u in n_u]
    gg = [mg[u][:, PAIR:] + kv[u] for u in n_u]

    st = [st_ref[p] for p in pairs]
    yst = []
    for ch in range(n_ch):
        base = ch * len(pairs)
        yst.append([_dot(rh[base + p], st[p]) + y0[base + p] for p in pairs])
        st = [_dot(mm[base + p], st[p]) + gg[base + p] for p in pairs]
    for p in pairs:
        st_ref[p] = st[p]
    y = [jnp.concatenate([yst[ch][p][:CHUNK] + yst[ch][p][CHUNK:] for ch in range(n_ch)], axis=0)
         for p in pairs]

    inv_n = 1.0 / RWKV_HEAD
    mean = [_dot(x_, seg_ones) * inv_n for x_ in y]
    dlt = [x_ - m_ for x_, m_ in zip(y, mean)]
    var = [_dot(x_ * x_, seg_ones) * inv_n for x_ in dlt]
    bonus = [_dot(r[:, sl] * k2[p] * rk_ref[:, sl], seg_ones) * v[:, sl] for p, sl in zip(pairs, sls)]
    for p, sl in zip(pairs, sls):
        yn = dlt[p] * lax.rsqrt(var[p] + LNX_EPS) * lnw_ref[:, sl] + lnb_ref[:, sl]
        o_ref[0, :, sl] = ((yn + bonus[p]) * g[:, sl]).astype(o_ref.dtype)


def _rwkv(pm3, pl3, prm, width):
    b, lp, _ = pm3.shape
    n_rkv = 3 * width
    vec = lambda n: pl.BlockSpec((1, n), lambda i, c: (0, 0))
    mat = lambda k: pl.BlockSpec((k, width), lambda i, c: (0, 0))
    return pl.pallas_call(
        functools.partial(_rwkv_body, width=width),
        grid=(b, lp // RWKV_ROWS),
        in_specs=[
            pl.BlockSpec((1, RWKV_ROWS, n_rkv), lambda i, c: (i, c, 1)),
            pl.BlockSpec((1, RWKV_ROWS, LORA_W), lambda i, c: (i, c, 0)),
            vec(n_rkv), vec(LORA_W), vec(width), mat(LANES), vec(width), mat(LANES),
            mat(2 * LANES), vec(width), vec(width), vec(width), vec(width), vec(width),
        ],
        out_specs=pl.BlockSpec((1, RWKV_ROWS, width), lambda i, c: (i, c, 0)),
        out_shape=jax.ShapeDtypeStruct((b, lp, width), BF16),
        scratch_shapes=[
            pltpu.VMEM((1, n_rkv), F32),
            pltpu.VMEM((1, LORA_W), F32),
            pltpu.VMEM((width // PAIR, PAIR, PAIR), F32),
        ],
        compiler_params=_params("parallel", "arbitrary"),
    )(pm3, pl3, prm["mu_rkv"], prm["mu_lora"], prm["w0"], prm["w_up"], prm["a0"], prm["a_up"],
      prm["g_up"], prm["k_k"], prm["k_a"], prm["r_k"], prm["lnx_w"], prm["lnx_b"])


def _gate_body(x_ref, bf_ref, o_ref, carry_ref, *, heads, lane0):
    @pl.when(pl.program_id(1) == 0)
    def _():
        carry_ref[...] = jnp.zeros_like(carry_ref)

    zz = x_ref[0] + bf_ref[...]
    log_f = jnp.minimum(zz, 0.0) - jnp.log1p(jnp.exp(-jnp.abs(zz)))
    ci = lax.broadcasted_iota(jnp.int32, (GATE_TB, GATE_TB), 0)
    cj = lax.broadcasted_iota(jnp.int32, (GATE_TB, GATE_TB), 1)
    csum = _dot_exact_lhs((cj <= ci).astype(BF16), log_f) + carry_ref[...]
    carry_ref[...] = csum[GATE_TB - 1:GATE_TB, :]
    pieces = _split3(csum * LOG2E)
    src = lax.broadcasted_iota(jnp.int32, (LANES, LANES), 0)
    dst = lax.broadcasted_iota(jnp.int32, (LANES, LANES), 1)
    acc = jnp.zeros((GATE_TB, LANES), F32)
    for i, piece in enumerate(pieces):
        move = ((dst % BIAS_PIECES == i) & (dst < BIAS_PIECES * heads)
                & (src == lane0 + dst // BIAS_PIECES)).astype(BF16)
        acc = acc + jnp.dot(piece, move, preferred_element_type=F32)
    o_ref[0] = acc.astype(BF16)


def _gate_cumsum(p3, bf_row, lane_blk, heads, lane0):
    b, lp, _ = p3.shape
    return pl.pallas_call(
        functools.partial(_gate_body, heads=heads, lane0=lane0),
        grid=(b, lp // GATE_TB),
        in_specs=[
            pl.BlockSpec((1, GATE_TB, LANES), lambda i, c: (i, c, lane_blk)),
            pl.BlockSpec((1, LANES), lambda i, c: (0, 0)),
        ],
        out_specs=pl.BlockSpec((1, GATE_TB, LANES), lambda i, c: (i, c, 0)),
        out_shape=jax.ShapeDtypeStruct((b, lp, LANES), BF16),
        scratch_shapes=[pltpu.VMEM((1, LANES), F32)],
        compiler_params=_params("parallel", "arbitrary"),
    )(p3, bf_row)


def _attn_body(q_ref, k_ref, vt_ref, cp_ref, o_ref, m_ref, l_ref, acc_ref, *, heads):
    qi = pl.program_id(1)
    m_ref[...] = jnp.full_like(m_ref, NEG_INF)
    l_ref[...] = jnp.zeros_like(l_ref)
    acc_ref[...] = jnp.zeros_like(acc_ref)
    hs = [slice(g * FOX_HEAD, (g + 1) * FOX_HEAD) for g in range(heads)]
    gs = range(heads)
    lane = lax.broadcasted_iota(jnp.int32, (ATT_T, LANES), 1)
    q_aug = [jnp.concatenate(
        [q_ref[0, :, hs[g]], -(lane // BIAS_PIECES == g).astype(BF16)], axis=1) for g in gs]

    def block(j, masked):
        off = pl.multiple_of(j * ATT_T, ATT_T)
        bias = cp_ref[0, pl.ds(off, ATT_T), :]
        s = [lax.dot_general(
            jnp.concatenate([k_ref[0, pl.ds(off, ATT_T), hs[g]], bias], axis=1),
            q_aug[g], _NT, preferred_element_type=F32) for g in gs]
        if masked:
            kpos = lax.broadcasted_iota(jnp.int32, (ATT_T, ATT_T), 0)
            qpos = lax.broadcasted_iota(jnp.int32, (ATT_T, ATT_T), 1)
            s = [jnp.where(kpos <= qpos, x, NEG_INF) for x in s]
        m_prev = [m_ref[g] for g in gs]
        m_new = [jnp.maximum(m_prev[g], jnp.max(s[g], axis=0, keepdims=True)) for g in gs]
        p = [jnp.exp2(s[g] - m_new[g]) for g in gs]
        alpha = [jnp.exp2(m_prev[g] - m_new[g]) for g in gs]
        pv = [jnp.dot(vt_ref[0, hs[g], pl.ds(off, ATT_T)], p[g].astype(BF16),
                      preferred_element_type=F32) for g in gs]
        for g in gs:
            l_ref[g] = alpha[g] * l_ref[g] + jnp.sum(p[g], axis=0, keepdims=True)
            acc_ref[hs[g], :] = alpha[g] * acc_ref[hs[g], :] + pv[g]
            m_ref[g] = m_new[g]

    def body(j, carry):
        block(j, False)
        return carry

    lax.fori_loop(0, qi, body, 0)
    block(qi, True)
    for g in gs:
        o_ref[0, :, hs[g]] = (acc_ref[hs[g], :] / l_ref[g]).T


def _attention(qkv3, vt3, cp, width):
    b, lp, _ = qkv3.shape
    heads = width // FOX_HEAD
    resident = pl.Buffered(1)
    return pl.pallas_call(
        functools.partial(_attn_body, heads=heads),
        grid=(b, lp // ATT_T),
        in_specs=[
            pl.BlockSpec((1, ATT_T, width), lambda i, q: (i, q, 0)),
            pl.BlockSpec((1, lp, width), lambda i, q: (i, 0, 1), pipeline_mode=resident),
            pl.BlockSpec((1, width, lp), lambda i, q: (i, 0, 0), pipeline_mode=resident),
            pl.BlockSpec((1, lp, LANES), lambda i, q: (i, 0, 0), pipeline_mode=resident),
        ],
        out_specs=pl.BlockSpec((1, ATT_T, width), lambda i, q: (i, q, 0)),
        out_shape=jax.ShapeDtypeStruct((b, lp, width), F32),
        scratch_shapes=[
            pltpu.VMEM((heads, 1, ATT_T), F32),
            pltpu.VMEM((heads, 1, ATT_T), F32),
            pltpu.VMEM((width, ATT_T), F32),
        ],
        compiler_params=_params("parallel", "arbitrary"),
    )(qkv3, qkv3, vt3, cp)


def _out_body(h_ref, yr_ref, yf_ref, gn_ref, w1_ref, w2_ref, o_ref):
    yf = _rms_scale(yf_ref[...], gn_ref[...])
    acc = jnp.dot(yr_ref[...], w1_ref[0], preferred_element_type=F32)
    acc = acc + jnp.dot(yf.astype(BF16), w2_ref[0], preferred_element_type=F32)
    o_ref[...] = h_ref[...] + acc


def _out_proj(h, y_rwkv, y_fox, fox_gain, w_out, layer):
    t, d = h.shape
    w1 = y_rwkv.shape[1]
    w2 = y_fox.shape[1]
    return pl.pallas_call(
        _out_body,
        grid=(t // OUT_TM,),
        in_specs=[
            pl.BlockSpec((OUT_TM, d), lambda i: (i, 0)),
            pl.BlockSpec((OUT_TM, w1), lambda i: (i, 0)),
            pl.BlockSpec((OUT_TM, w2), lambda i: (i, 0)),
            pl.BlockSpec((1, w2), lambda i: (0, 0)),
            pl.BlockSpec((1, w1, d), lambda i: (layer, 0, 0)),
            pl.BlockSpec((1, w2, d), lambda i: (layer, 1, 0)),
        ],
        out_specs=pl.BlockSpec((OUT_TM, d), lambda i: (i, 0)),
        out_shape=jax.ShapeDtypeStruct((t, d), F32),
        compiler_params=_params("parallel"),
    )(h, y_rwkv, y_fox, fox_gain, w_out, w_out)


def _norm_body(a_ref, b_ref, g_ref, o_ref):
    x = jnp.concatenate([a_ref[0, N_META:, :], b_ref[0]], axis=0)
    o_ref[0] = _rms_scale(x, g_ref[...])


def _final_norm(h3, gain, seq):
    b, _, d = h3.shape
    per = NORM_TM // N_META
    return pl.pallas_call(
        _norm_body,
        grid=(b, seq // NORM_TM),
        in_specs=[
            pl.BlockSpec((1, NORM_TM, d), lambda i, r: (i, r, 0)),
            pl.BlockSpec((1, N_META, d), lambda i, r: (i, (r + 1) * per, 0)),
            pl.BlockSpec((1, d), lambda i, r: (0, 0)),
        ],
        out_specs=pl.BlockSpec((1, NORM_TM, d), lambda i, r: (i, r, 0)),
        out_shape=jax.ShapeDtypeStruct((b, seq, d), F32),
        compiler_params=_params("parallel", "parallel"),
    )(h3, h3, gain)


def _pad_rows(w, rows):
    return jnp.pad(w, ((0, rows - w.shape[0]), (0, 0)))


def _pack_w_in(w_in, width, heads):
    d = w_in.shape[0]
    c3 = 3 * width
    c4, c5, c6 = c3 + W_LORA, c3 + W_LORA + A_LORA, c3 + W_LORA + A_LORA + G_LORA
    lora = jnp.zeros((d, LORA_W), w_in.dtype)
    lora = lora.at[:, 0:W_LORA].set(w_in[:, c3:c4])
    lora = lora.at[:, LANES:LANES + A_LORA].set(w_in[:, c4:c5])
    lora = lora.at[:, 2 * LANES:2 * LANES + G_LORA].set(w_in[:, c5:c6])
    lora = lora.at[:, F_LANE:F_LANE + heads].set(w_in[:, c6 + c3:c6 + c3 + heads])
    return jnp.concatenate([w_in[:, c6:c6 + c3], w_in[:, :c3], lora], axis=1).astype(BF16)


def _pack_mu_lora(mu, width):
    c3 = 3 * width
    c4, c5, c6 = c3 + W_LORA, c3 + W_LORA + A_LORA, c3 + W_LORA + A_LORA + G_LORA
    out = jnp.zeros((LORA_W,), mu.dtype)
    out = out.at[0:W_LORA].set(mu[c3:c4])
    out = out.at[LANES:LANES + A_LORA].set(mu[c4:c5])
    out = out.at[2 * LANES:2 * LANES + G_LORA].set(mu[c5:c6])
    return out[None, :]


def kernel(x, meta_tokens, ffn1_norm, ffn1_w_gu, ffn1_w_down, mix_norm, w_in, rwkv_mu, rwkv_w0,
           rwkv_w_up, rwkv_a0, rwkv_a_up, rwkv_g_up, rwkv_k_k, rwkv_k_a, rwkv_r_k, rwkv_lnx_w,
           rwkv_lnx_b, fox_b_f, fox_out_norm, w_out, ffn2_norm, ffn2_w_gu, ffn2_w_down, final_norm):
    b, seq, d = x.shape
    depth = w_in.shape[0]
    width = rwkv_w0.shape[1]
    heads = fox_b_f.shape[1]
    assert rwkv_g_up.shape[1] == G_LORA and G_LORA <= 2 * LANES
    assert F_LANE >= 2 * LANES + G_LORA and F_LANE + heads <= LORA_W and F_LANE % LANES + heads <= LANES
    l = N_META + seq
    lp = -(-l // TIME_ALIGN) * TIME_ALIGN
    t = b * lp
    assert t % PROJ_TM == 0 and t % FFN_TM == 0 and lp % RWKV_ROWS == 0 and RWKV_ROWS % CHUNK == 0
    assert width % PROJ_TN == 0 and LORA_W == PROJ_TN and BIAS_PIECES * heads <= LANES

    meta = jnp.broadcast_to(meta_tokens.astype(x.dtype)[None], (b, N_META, d))
    h = jnp.concatenate([meta, x, jnp.zeros((b, lp - l, d), x.dtype)], axis=1).reshape(t, d)

    row = lambda vct: vct.astype(F32)[None, :]
    f_blk = F_LANE // LANES
    f_lane0 = F_LANE % LANES
    wgu1, wdn1 = ffn1_w_gu.astype(BF16), ffn1_w_down.astype(BF16)
    wgu2, wdn2 = ffn2_w_gu.astype(BF16), ffn2_w_down.astype(BF16)
    w_out_bf = w_out.astype(BF16)
    for i in range(depth):
        h = _ffn(h, row(ffn1_norm[i]), wgu1, wdn1, i)
        p_main, p_lora = _in_proj(h, row(mix_norm[i]), _pack_w_in(w_in[i], width, heads),
                                  scaled_blocks=width // PROJ_TN, scale=FOX_HEAD ** -0.5 * LOG2E)
        pm3 = p_main.reshape(b, lp, 6 * width)
        pl3 = p_lora.reshape(b, lp, LORA_W)
        prm = dict(
            mu_rkv=row(rwkv_mu[i, :3 * width]), mu_lora=_pack_mu_lora(rwkv_mu[i], width),
            w0=row(rwkv_w0[i]), w_up=_pad_rows(rwkv_w_up[i], LANES).astype(BF16),
            a0=row(rwkv_a0[i]), a_up=_pad_rows(rwkv_a_up[i], LANES).astype(BF16),
            g_up=_pad_rows(rwkv_g_up[i], 2 * LANES).astype(BF16),
            k_k=row(rwkv_k_k[i]), k_a=row(rwkv_k_a[i]), r_k=row(rwkv_r_k[i].reshape(-1)),
            lnx_w=row(rwkv_lnx_w[i]), lnx_b=row(rwkv_lnx_b[i]))
        y_rwkv = _rwkv(pm3, pl3, prm, width)
        bf_row = jnp.zeros((1, LANES), F32).at[0, f_lane0:f_lane0 + heads].set(fox_b_f[i])
        cp = _gate_cumsum(pl3, bf_row, f_blk, heads, f_lane0)
        vt3 = jnp.transpose(pm3[:, :, 2 * width:3 * width], (0, 2, 1))
        y_fox = _attention(pm3, vt3, cp, width)
        h = _out_proj(h, y_rwkv.reshape(t, width), y_fox.reshape(t, width), row(fox_out_norm[i]),
                      w_out_bf, i)
        h = _ffn(h, row(ffn2_norm[i]), wgu2, wdn2, i)
    assert seq % NORM_TM == 0 and NORM_TM % N_META == 0 and N_META % 8 == 0
    return _final_norm(h.reshape(b, lp, d), row(final_norm), seq)
```

```python
import functools
import math

import jax
import jax.numpy as jnp
from jax import lax
from jax.experimental import pallas as pl
from jax.experimental.pallas import tpu as pltpu

F32 = jnp.float32
BF16 = jnp.bfloat16

N_META = 16
RWKV_HEAD = 64
FOX_HEAD = 128
W_LORA = 64
A_LORA = 64
G_LORA = 160
NORM_EPS = 1e-6
LNX_EPS = 64e-5
L2_EPS = 1e-12
NEG_INF = -1e30
LOG2E = math.log2(math.e)
BIAS_PIECES = 3

LANES = 128
VMEM_LIMIT_BYTES = 56 * 1024 * 1024

CHUNK = 64
RWKV_ROWS = 192
PAIR = 2 * RWKV_HEAD
TIME_ALIGN = 384
ATT_T = 384
GATE_TB = 384
FFN_TM = 1024
FFN_TF = 512
NORM_TM = 512
PROJ_TM = 1056
PROJ_TN = 1024
OUT_TM = 512
LORA_W = 512
F_LANE = 448

_NT = (((1,), (1,)), ((), ()))


def _rms_scale(x, gain):
    ms = jnp.mean(x * x, axis=-1, keepdims=True)
    return x * lax.rsqrt(ms + NORM_EPS) * gain


def _dot(a, b):
    return jnp.dot(a.astype(BF16), b.astype(BF16), preferred_element_type=F32)


def _dot_nt(a, b):
    return lax.dot_general(a.astype(BF16), b.astype(BF16), _NT, preferred_element_type=F32)


def _split3(x):
    hi = x.astype(BF16)
    r1 = x - hi.astype(F32)
    mid = r1.astype(BF16)
    lo = (r1 - mid.astype(F32)).astype(BF16)
    return hi, mid, lo


def _dot_exact_lhs(m_bf16, x):
    hi, mid, lo = _split3(x)
    acc = jnp.dot(m_bf16, lo, preferred_element_type=F32)
    acc = acc + jnp.dot(m_bf16, mid, preferred_element_type=F32)
    return acc + jnp.dot(m_bf16, hi, preferred_element_type=F32)


def _params(*sem):
    return pltpu.CompilerParams(dimension_semantics=sem, vmem_limit_bytes=VMEM_LIMIT_BYTES)


def _ffn_body(h_ref, g_ref, wg_ref, wu_ref, wd_ref, o_ref, u_ref):
    f = pl.program_id(1)

    @pl.when(f == 0)
    def _():
        x = h_ref[...]
        u_ref[...] = _rms_scale(x, g_ref[...]).astype(BF16)
        o_ref[...] = x

    u = u_ref[...]
    gate = jnp.dot(u, wg_ref[0], preferred_element_type=F32)
    up = jnp.dot(u, wu_ref[0], preferred_element_type=F32)
    act = (0.5 * gate) * jax.nn.sigmoid(gate) * up
    o_ref[...] += jnp.dot(act.astype(BF16), wd_ref[0], preferred_element_type=F32)


def _ffn(h, gain, w_gu, w_down, layer):
    t, d = h.shape
    ff = w_down.shape[1]
    nf = ff // FFN_TF
    return pl.pallas_call(
        _ffn_body,
        grid=(t // FFN_TM, nf),
        in_specs=[
            pl.BlockSpec((FFN_TM, d), lambda i, f: (i, 0)),
            pl.BlockSpec((1, d), lambda i, f: (0, 0)),
            pl.BlockSpec((1, d, FFN_TF), lambda i, f: (layer, 0, f)),
            pl.BlockSpec((1, d, FFN_TF), lambda i, f: (layer, 0, f + nf)),
            pl.BlockSpec((1, FFN_TF, d), lambda i, f: (layer, f, 0)),
        ],
        out_specs=pl.BlockSpec((FFN_TM, d), lambda i, f: (i, 0)),
        out_shape=jax.ShapeDtypeStruct((t, d), F32),
        scratch_shapes=[pltpu.VMEM((FFN_TM, d), BF16)],
        compiler_params=_params("parallel", "arbitrary"),
    )(h, gain, w_gu, w_gu, w_down)


def _proj_body(h_ref, g_ref, w_ref, wl_ref, o_ref, ol_ref, u_ref, *, scaled_blocks, scale):
    j = pl.program_id(1)

    @pl.when(j == 0)
    def _():
        u_ref[...] = _rms_scale(h_ref[...], g_ref[...]).astype(BF16)

    acc = jnp.dot(u_ref[...], w_ref[...], preferred_element_type=F32)
    o_ref[...] = (acc * jnp.where(j < scaled_blocks, scale, 1.0)).astype(o_ref.dtype)

    @pl.when(j == pl.num_programs(1) - 1)
    def _():
        ol_ref[...] = jnp.dot(u_ref[...], wl_ref[...], preferred_element_type=F32)


def _in_proj(h, gain, w, w_lora, scaled_blocks, scale):
    t, d = h.shape
    n = w.shape[1]
    nl = w_lora.shape[1]
    return pl.pallas_call(
        functools.partial(_proj_body, scaled_blocks=scaled_blocks, scale=scale),
        grid=(t // PROJ_TM, n // PROJ_TN),
        in_specs=[
            pl.BlockSpec((PROJ_TM, d), lambda i, j: (i, 0)),
            pl.BlockSpec((1, d), lambda i, j: (0, 0)),
            pl.BlockSpec((d, PROJ_TN), lambda i, j: (0, j)),
            pl.BlockSpec((d, nl), lambda i, j: (0, 0)),
        ],
        out_specs=[
            pl.BlockSpec((PROJ_TM, PROJ_TN), lambda i, j: (i, j)),
            pl.BlockSpec((PROJ_TM, nl), lambda i, j: (i, 0)),
        ],
        out_shape=[jax.ShapeDtypeStruct((t, n), BF16), jax.ShapeDtypeStruct((t, nl), F32)],
        scratch_shapes=[pltpu.VMEM((PROJ_TM, d), BF16)],
        compiler_params=_params("parallel", "arbitrary"),
    )(h, gain, w, w_lora)


def _rwkv_body(rkv_ref, lora_ref, mu_rkv_ref, mu_lora_ref, w0_ref, wup_ref, a0_ref, aup_ref,
               gup_ref, kk_ref, ka_ref, rk_ref, lnw_ref, lnb_ref, o_ref,
               carry_rkv, carry_lora, st_ref, *, width):
    rows = RWKV_ROWS
    n_ch = rows // CHUNK
    c = pl.program_id(1)

    @pl.when(c == 0)
    def _():
        carry_rkv[...] = jnp.zeros_like(carry_rkv)
        carry_lora[...] = jnp.zeros_like(carry_lora)
        st_ref[...] = jnp.zeros_like(st_ref)

    row = lax.broadcasted_iota(jnp.int32, (rows, 1), 0)

    def shift(p, carry_ref, mu):
        prev = jnp.where(row == 0, carry_ref[...], pltpu.roll(p, 1, 0))
        carry_ref[...] = p[rows - 1:rows, :]
        return p + (prev - p) * mu

    pf = shift(rkv_ref[0].astype(F32), carry_rkv, mu_rkv_ref[...])
    lf = shift(lora_ref[0], carry_lora, mu_lora_ref[...])
    r = pf[:, :width]
    k = pf[:, width:2 * width]
    v = pf[:, 2 * width:]
    wd = lf[:, 0:LANES]
    ad = lf[:, LANES:2 * LANES]
    gd = lf[:, 2 * LANES:4 * LANES]

    z = w0_ref[...] + _dot(jnp.tanh(wd), wup_ref[...])
    nz = -z
    softplus = jnp.maximum(nz, 0.0) + jnp.log1p(jnp.exp(-jnp.abs(nz)))
    logw = -jnp.exp(-softplus - 0.5)
    a = jax.nn.sigmoid(a0_ref[...] + _dot(ad, aup_ref[...]))
    g = _dot(jax.nn.sigmoid(gd), gup_ref[...])

    ci = lax.broadcasted_iota(jnp.int32, (rows, rows), 0)
    cj = lax.broadcasted_iota(jnp.int32, (rows, rows), 1)
    ltri = ((cj <= ci) & ((ci // CHUNK) == (cj // CHUNK))).astype(BF16)
    cs = _dot_exact_lhs(ltri, logw)

    ii = lax.broadcasted_iota(jnp.int32, (PAIR, PAIR), 0)
    jj = lax.broadcasted_iota(jnp.int32, (PAIR, PAIR), 1)
    same_head = (ii // RWKV_HEAD) == (jj // RWKV_HEAD)
    mask_strict = same_head & (jj < ii)
    mask_incl = same_head & (jj <= ii)
    eye = ii == jj
    seg_ones = same_head.astype(BF16)
    lane = lax.broadcasted_iota(jnp.int32, (CHUNK, PAIR), 1)
    head0 = lane < RWKV_HEAD

    def stack(x):
        return jnp.concatenate([jnp.where(head0, x, 0.0), jnp.where(head0, 0.0, x)], axis=0)

    def both(x):
        return jnp.concatenate([x, x], axis=0)

    pairs = range(width // PAIR)
    sls = [slice(hp * PAIR, (hp + 1) * PAIR) for hp in pairs]
    chunks = [slice(ch * CHUNK, (ch + 1) * CHUNK) for ch in range(n_ch)]
    units = [(ch, p) for ch in range(n_ch) for p in pairs]

    kk = [k[:, sl] * kk_ref[:, sl] for sl in sls]
    ss = [_dot(x * x, seg_ones) for x in kk]
    kk = [x / jnp.maximum(jnp.sqrt(s), L2_EPS) for x, s in zip(kk, ss)]
    k2 = [k[:, sl] * (1.0 + (a[:, sl] - 1.0) * ka_ref[:, sl]) for sl in sls]
    bv = [x * a[:, sl] for x, sl in zip(kk, sls)]
    e_neg = [jnp.exp(-cs[:, sl]) for sl in sls]
    at_ = [-kk[p] * jnp.exp(cs[:, sls[p]] - logw[:, sls[p]]) for p in pairs]
    rt_ = [r[:, sl] * jnp.exp(cs[:, sl]) for sl in sls]
    bt_ = [bv[p] * e_neg[p] for p in pairs]
    kt_ = [k2[p] * e_neg[p] for p in pairs]

    last = [cs[chunks[ch]][CHUNK - 1:CHUNK, sls[p]] for ch, p in units]
    e_end = [jnp.exp(last[u] - cs[chunks[ch], sls[p]]) for u, (ch, p) in enumerate(units)]
    xa = [stack(at_[p][chunks[ch]]) for ch, p in units]
    xr = [stack(rt_[p][chunks[ch]]) for ch, p in units]
    xbh_t = [stack(bv[p][chunks[ch]] * e_end[u]).T for u, (ch, p) in enumerate(units)]
    xkh_t = [stack(k2[p][chunks[ch]] * e_end[u]).T for u, (ch, p) in enumerate(units)]
    vst = [stack(v[chunks[ch], sls[p]]) for ch, p in units]

    big = [_dot_nt(jnp.concatenate([xa[u], xr[u]], axis=0),
                   jnp.concatenate([both(bt_[p][chunks[ch]]), both(kt_[p][chunks[ch]])], axis=0))
           for u, (ch, p) in enumerate(units)]
    big = [x.astype(BF16) for x in big]
    zero = jnp.zeros((PAIR, PAIR), BF16)

    def packed(mask):
        return mask.astype(BF16) != 0

    m_strict, m_incl = packed(mask_strict), packed(mask_incl)
    a_ab = [x[:PAIR, :PAIR] for x in big]
    a_ak = [jnp.where(m_strict, x[:PAIR, PAIR:], zero) for x in big]
    a_rb = [jnp.where(m_incl, x[PAIR:, :PAIR], zero) for x in big]
    a_rk = [jnp.where(m_incl, x[PAIR:, PAIR:], zero) for x in big]

    lvl1 = packed(((ii ^ jj) == 1) & (jj < ii))
    eye_bf = eye.astype(BF16)
    tinv = [eye_bf + jnp.where(lvl1, x, zero) for x in a_ab]
    s = 2
    while s < RWKV_HEAD:
        e_mask = packed(((ii // (2 * s)) == (jj // (2 * s))) & (((ii // s) % 2) == 1)
                        & (((jj // s) % 2) == 0))
        half = [_dot(t_, jnp.where(e_mask, x, zero)).astype(BF16) for t_, x in zip(tinv, a_ab)]
        tinv = [t_ + _dot(h_, t_).astype(BF16) for t_, h_ in zip(tinv, half)]
        s *= 2

    n_u = range(len(units))
    w_av = [_dot(a_ak[u], vst[u]) for u in n_u]
    x = [_dot(tinv[u], jnp.concatenate([xa[u], w_av[u]], axis=1)) for u in n_u]
    mg = [_dot(xbh_t[u], x[u]) for u in n_u]
    kv = [_dot(xkh_t[u], vst[u]) for u in n_u]
    ry = [_dot(a_rb[u], x[u]) for u in n_u]
    rkv_ = [_dot(a_rk[u], vst[u]) for u in n_u]
    rh = [xr[u] + ry[u][:, :PAIR] for u in n_u]
    y0 = [ry[u][:, PAIR:] + rkv_[u] for u in n_u]
    mm = [mg[u][:, :PAIR] + jnp.where(eye, jnp.exp(last[u]), 0.0) for u in n_u]
    gg = [mg[u][:, PAIR:] + kv[u] for u in n_u]

    st = [st_ref[p] for p in pairs]
    yst = []
    for ch in range(n_ch):
        base = ch * len(pairs)
        yst.append([_dot(rh[base + p], st[p]) + y0[base + p] for p in pairs])
        st = [_dot(mm[base + p], st[p]) + gg[base + p] for p in pairs]
    for p in pairs:
        st_ref[p] = st[p]
    y = [jnp.concatenate([yst[ch][p][:CHUNK] + yst[ch][p][CHUNK:] for ch in range(n_ch)], axis=0)
         for p in pairs]

    inv_n = 1.0 / RWKV_HEAD
    mean = [_dot(x_, seg_ones) * inv_n for x_ in y]
    dlt = [x_ - m_ for x_, m_ in zip(y, mean)]
    var = [_dot(x_ * x_, seg_ones) * inv_n for x_ in dlt]
    bonus = [_dot(r[:, sl] * k2[p] * rk_ref[:, sl], seg_ones) * v[:, sl] for p, sl in zip(pairs, sls)]
    for p, sl in zip(pairs, sls):
        yn = dlt[p] * lax.rsqrt(var[p] + LNX_EPS) * lnw_ref[:, sl] + lnb_ref[:, sl]
        o_ref[0, :, sl] = ((yn + bonus[p]) * g[:, sl]).astype(o_ref.dtype)


def _rwkv(pm3, pl3, prm, width):
    b, lp, _ = pm3.shape
    n_rkv = 3 * width
    vec = lambda n: pl.BlockSpec((1, n), lambda i, c: (0, 0))
    mat = lambda k: pl.BlockSpec((k, width), lambda i, c: (0, 0))
    return pl.pallas_call(
        functools.partial(_rwkv_body, width=width),
        grid=(b, lp // RWKV_ROWS),
        in_specs=[
            pl.BlockSpec((1, RWKV_ROWS, n_rkv), lambda i, c: (i, c, 1)),
            pl.BlockSpec((1, RWKV_ROWS, LORA_W), lambda i, c: (i, c, 0)),
            vec(n_rkv), vec(LORA_W), vec(width), mat(LANES), vec(width), mat(LANES),
            mat(2 * LANES), vec(width), vec(width), vec(width), vec(width), vec(width),
        ],
        out_specs=pl.BlockSpec((1, RWKV_ROWS, width), lambda i, c: (i, c, 0)),
        out_shape=jax.ShapeDtypeStruct((b, lp, width), BF16),
        scratch_shapes=[
            pltpu.VMEM((1, n_rkv), F32),
            pltpu.VMEM((1, LORA_W), F32),
            pltpu.VMEM((width // PAIR, PAIR, PAIR), F32),
        ],
        compiler_params=_params("parallel", "arbitrary"),
    )(pm3, pl3, prm["mu_rkv"], prm["mu_lora"], prm["w0"], prm["w_up"], prm["a0"], prm["a_up"],
      prm["g_up"], prm["k_k"], prm["k_a"], prm["r_k"], prm["lnx_w"], prm["lnx_b"])


def _gate_body(x_ref, bf_ref, o_ref, carry_ref, *, heads, lane0):
    @pl.when(pl.program_id(1) == 0)
    def _():
        carry_ref[...] = jnp.zeros_like(carry_ref)

    zz = x_ref[0] + bf_ref[...]
    log_f = jnp.minimum(zz, 0.0) - jnp.log1p(jnp.exp(-jnp.abs(zz)))
    ci = lax.broadcasted_iota(jnp.int32, (GATE_TB, GATE_TB), 0)
    cj = lax.broadcasted_iota(jnp.int32, (GATE_TB, GATE_TB), 1)
    csum = _dot_exact_lhs((cj <= ci).astype(BF16), log_f) + carry_ref[...]
    carry_ref[...] = csum[GATE_TB - 1:GATE_TB, :]
    pieces = _split3(csum * LOG2E)
    src = lax.broadcasted_iota(jnp.int32, (LANES, LANES), 0)
    dst = lax.broadcasted_iota(jnp.int32, (LANES, LANES), 1)
    acc = jnp.zeros((GATE_TB, LANES), F32)
    for i, piece in enumerate(pieces):
        move = ((dst % BIAS_PIECES == i) & (dst < BIAS_PIECES * heads)
                & (src == lane0 + dst // BIAS_PIECES)).astype(BF16)
        acc = acc + jnp.dot(piece, move, preferred_element_type=F32)
    o_ref[0] = acc.astype(BF16)


def _gate_cumsum(p3, bf_row, lane_blk, heads, lane0):
    b, lp, _ = p3.shape
    return pl.pallas_call(
        functools.partial(_gate_body, heads=heads, lane0=lane0),
        grid=(b, lp // GATE_TB),
        in_specs=[
            pl.BlockSpec((1, GATE_TB, LANES), lambda i, c: (i, c, lane_blk)),
            pl.BlockSpec((1, LANES), lambda i, c: (0, 0)),
        ],
        out_specs=pl.BlockSpec((1, GATE_TB, LANES), lambda i, c: (i, c, 0)),
        out_shape=jax.ShapeDtypeStruct((b, lp, LANES), BF16),
        scratch_shapes=[pltpu.VMEM((1, LANES), F32)],
        compiler_params=_params("parallel", "arbitrary"),
    )(p3, bf_row)


def _attn_body(q_ref, k_ref, vt_ref, cp_ref, o_ref, m_ref, l_ref, acc_ref, *, heads):
    qi = pl.program_id(1)
    m_ref[...] = jnp.full_like(m_ref, NEG_INF)
    l_ref[...] = jnp.zeros_like(l_ref)
    acc_ref[...] = jnp.zeros_like(acc_ref)
    hs = [slice(g * FOX_HEAD, (g + 1) * FOX_HEAD) for g in range(heads)]
    gs = range(heads)
    lane = lax.broadcasted_iota(jnp.int32, (ATT_T, LANES), 1)
    q_aug = [jnp.concatenate(
        [q_ref[0, :, hs[g]], -(lane // BIAS_PIECES == g).astype(BF16)], axis=1) for g in gs]

    def block(j, masked):
        off = pl.multiple_of(j * ATT_T, ATT_T)
        bias = cp_ref[0, pl.ds(off, ATT_T), :]
        s = [lax.dot_general(
            jnp.concatenate([k_ref[0, pl.ds(off, ATT_T), hs[g]], bias], axis=1),
            q_aug[g], _NT, preferred_element_type=F32) for g in gs]
        if masked:
            kpos = lax.broadcasted_iota(jnp.int32, (ATT_T, ATT_T), 0)
            qpos = lax.broadcasted_iota(jnp.int32, (ATT_T, ATT_T), 1)
            s = [jnp.where(kpos <= qpos, x, NEG_INF) for x in s]
        m_prev = [m_ref[g] for g in gs]
        m_new = [jnp.maximum(m_prev[g], jnp.max(s[g], axis=0, keepdims=True)) for g in gs]
        p = [jnp.exp2(s[g] - m_new[g]) for g in gs]
        alpha = [jnp.exp2(m_prev[g] - m_new[g]) for g in gs]
        pv = [jnp.dot(vt_ref[0, hs[g], pl.ds(off, ATT_T)], p[g].astype(BF16),
                      preferred_element_type=F32) for g in gs]
        for g in gs:
            l_ref[g] = alpha[g] * l_ref[g] + jnp.sum(p[g], axis=0, keepdims=True)
            acc_ref[hs[g], :] = alpha[g] * acc_ref[hs[g], :] + pv[g]
            m_ref[g] = m_new[g]

    def body(j, carry):
        block(j, False)
        return carry

    lax.fori_loop(0, qi, body, 0)
    block(qi, True)
    for g in gs:
        o_ref[0, :, hs[g]] = (acc_ref[hs[g], :] / l_ref[g]).T


def _attention(qkv3, vt3, cp, width):
    b, lp, _ = qkv3.shape
    heads = width // FOX_HEAD
    resident = pl.Buffered(1)
    return pl.pallas_call(
        functools.partial(_attn_body, heads=heads),
        grid=(b, lp // ATT_T),
        in_specs=[
            pl.BlockSpec((1, ATT_T, width), lambda i, q: (i, q, 0)),
            pl.BlockSpec((1, lp, width), lambda i, q: (i, 0, 1), pipeline_mode=resident),
            pl.BlockSpec((1, width, lp), lambda i, q: (i, 0, 0), pipeline_mode=resident),
            pl.BlockSpec((1, lp, LANES), lambda i, q: (i, 0, 0), pipeline_mode=resident),
        ],
        out_specs=pl.BlockSpec((1, ATT_T, width), lambda i, q: (i, q, 0)),
        out_shape=jax.ShapeDtypeStruct((b, lp, width), F32),
        scratch_shapes=[
            pltpu.VMEM((heads, 1, ATT_T), F32),
            pltpu.VMEM((heads, 1, ATT_T), F32),
            pltpu.VMEM((width, ATT_T), F32),
        ],
        compiler_params=_params("parallel", "arbitrary"),
    )(qkv3, qkv3, vt3, cp)


def _out_body(h_ref, yr_ref, yf_ref, gn_ref, w1_ref, w2_ref, o_ref):
    yf = _rms_scale(yf_ref[...], gn_ref[...])
    acc = jnp.dot(yr_ref[...], w1_ref[0], preferred_element_type=F32)
    acc = acc + jnp.dot(yf.astype(BF16), w2_ref[0], preferred_element_type=F32)
    o_ref[...] = h_ref[...] + acc


def _out_proj(h, y_rwkv, y_fox, fox_gain, w_out, layer):
    t, d = h.shape
    w1 = y_rwkv.shape[1]
    w2 = y_fox.shape[1]
    return pl.pallas_call(
        _out_body,
        grid=(t // OUT_TM,),
        in_specs=[
            pl.BlockSpec((OUT_TM, d), lambda i: (i, 0)),
            pl.BlockSpec((OUT_TM, w1), lambda i: (i, 0)),
            pl.BlockSpec((OUT_TM, w2), lambda i: (i, 0)),
            pl.BlockSpec((1, w2), lambda i: (0, 0)),
            pl.BlockSpec((1, w1, d), lambda i: (layer, 0, 0)),
            pl.BlockSpec((1, w2, d), lambda i: (layer, 1, 0)),
        ],
        out_specs=pl.BlockSpec((OUT_TM, d), lambda i: (i, 0)),
        out_shape=jax.ShapeDtypeStruct((t, d), F32),
        compiler_params=_params("parallel"),
    )(h, y_rwkv, y_fox, fox_gain, w_out, w_out)


def _norm_body(a_ref, b_ref, g_ref, o_ref):
    x = jnp.concatenate([a_ref[0, N_META:, :], b_ref[0]], axis=0)
    o_ref[0] = _rms_scale(x, g_ref[...])


def _final_norm(h3, gain, seq):
    b, _, d = h3.shape
    per = NORM_TM // N_META
    return pl.pallas_call(
        _norm_body,
        grid=(b, seq // NORM_TM),
        in_specs=[
            pl.BlockSpec((1, NORM_TM, d), lambda i, r: (i, r, 0)),
            pl.BlockSpec((1, N_META, d), lambda i, r: (i, (r + 1) * per, 0)),
            pl.BlockSpec((1, d), lambda i, r: (0, 0)),
        ],
        out_specs=pl.BlockSpec((1, NORM_TM, d), lambda i, r: (i, r, 0)),
        out_shape=jax.ShapeDtypeStruct((b, seq, d), F32),
        compiler_params=_params("parallel", "parallel"),
    )(h3, h3, gain)


def _pad_rows(w, rows):
    return jnp.pad(w, ((0, rows - w.shape[0]), (0, 0)))


def _pack_w_in(w_in, width, heads):
    d = w_in.shape[0]
    c3 = 3 * width
    c4, c5, c6 = c3 + W_LORA, c3 + W_LORA + A_LORA, c3 + W_LORA + A_LORA + G_LORA
    lora = jnp.zeros((d, LORA_W), w_in.dtype)
    lora = lora.at[:, 0:W_LORA].set(w_in[:, c3:c4])
    lora = lora.at[:, LANES:LANES + A_LORA].set(w_in[:, c4:c5])
    lora = lora.at[:, 2 * LANES:2 * LANES + G_LORA].set(w_in[:, c5:c6])
    lora = lora.at[:, F_LANE:F_LANE + heads].set(w_in[:, c6 + c3:c6 + c3 + heads])
    return jnp.concatenate([w_in[:, c6:c6 + c3], w_in[:, :c3]], axis=1).astype(BF16), lora.astype(BF16)


def _pack_mu_lora(mu, width):
    c3 = 3 * width
    c4, c5, c6 = c3 + W_LORA, c3 + W_LORA + A_LORA, c3 + W_LORA + A_LORA + G_LORA
    out = jnp.zeros((LORA_W,), mu.dtype)
    out = out.at[0:W_LORA].set(mu[c3:c4])
    out = out.at[LANES:LANES + A_LORA].set(mu[c4:c5])
    out = out.at[2 * LANES:2 * LANES + G_LORA].set(mu[c5:c6])
    return out[None, :]


def kernel(x, meta_tokens, ffn1_norm, ffn1_w_gu, ffn1_w_down, mix_norm, w_in, rwkv_mu, rwkv_w0,
           rwkv_w_up, rwkv_a0, rwkv_a_up, rwkv_g_up, rwkv_k_k, rwkv_k_a, rwkv_r_k, rwkv_lnx_w,
           rwkv_lnx_b, fox_b_f, fox_out_norm, w_out, ffn2_norm, ffn2_w_gu, ffn2_w_down, final_norm):
    b, seq, d = x.shape
    depth = w_in.shape[0]
    width = rwkv_w0.shape[1]
    heads = fox_b_f.shape[1]
    assert rwkv_g_up.shape[1] == G_LORA and G_LORA <= 2 * LANES
    assert F_LANE >= 2 * LANES + G_LORA and F_LANE + heads <= LORA_W and F_LANE % LANES + heads <= LANES
    l = N_META + seq
    lp = -(-l // TIME_ALIGN) * TIME_ALIGN
    t = b * lp
    assert t % PROJ_TM == 0 and t % FFN_TM == 0 and lp % RWKV_ROWS == 0 and RWKV_ROWS % CHUNK == 0
    assert width % PROJ_TN == 0 and BIAS_PIECES * heads <= LANES

    meta = jnp.broadcast_to(meta_tokens.astype(x.dtype)[None], (b, N_META, d))
    h = jnp.concatenate([meta, x, jnp.zeros((b, lp - l, d), x.dtype)], axis=1).reshape(t, d)

    row = lambda vct: vct.astype(F32)[None, :]
    f_blk = F_LANE // LANES
    f_lane0 = F_LANE % LANES
    wgu1, wdn1 = ffn1_w_gu.astype(BF16), ffn1_w_down.astype(BF16)
    wgu2, wdn2 = ffn2_w_gu.astype(BF16), ffn2_w_down.astype(BF16)
    w_out_bf = w_out.astype(BF16)
    for i in range(depth):
        h = _ffn(h, row(ffn1_norm[i]), wgu1, wdn1, i)
        p_main, p_lora = _in_proj(h, row(mix_norm[i]), *_pack_w_in(w_in[i], width, heads),
                                  scaled_blocks=width // PROJ_TN, scale=FOX_HEAD ** -0.5 * LOG2E)
        pm3 = p_main.reshape(b, lp, 6 * width)
        pl3 = p_lora.reshape(b, lp, LORA_W)
        prm = dict(
            mu_rkv=row(rwkv_mu[i, :3 * width]), mu_lora=_pack_mu_lora(rwkv_mu[i], width),
            w0=row(rwkv_w0[i]), w_up=_pad_rows(rwkv_w_up[i], LANES).astype(BF16),
            a0=row(rwkv_a0[i]), a_up=_pad_rows(rwkv_a_up[i], LANES).astype(BF16),
            g_up=_pad_rows(rwkv_g_up[i], 2 * LANES).astype(BF16),
            k_k=row(rwkv_k_k[i]), k_a=row(rwkv_k_a[i]), r_k=row(rwkv_r_k[i].reshape(-1)),
            lnx_w=row(rwkv_lnx_w[i]), lnx_b=row(rwkv_lnx_b[i]))
        y_rwkv = _rwkv(pm3, pl3, prm, width)
        bf_row = jnp.zeros((1, LANES), F32).at[0, f_lane0:f_lane0 + heads].set(fox_b_f[i])
        cp = _gate_cumsum(pl3, bf_row, f_blk, heads, f_lane0)
        vt3 = jnp.transpose(pm3[:, :, 2 * width:3 * width], (0, 2, 1))
        y_fox = _attention(pm3, vt3, cp, width)
        h = _out_proj(h, y_rwkv.reshape(t, width), y_fox.reshape(t, width), row(fox_out_norm[i]),
                      w_out_bf, i)
        h = _ffn(h, row(ffn2_norm[i]), wgu2, wdn2, i)
    assert seq % NORM_TM == 0 and NORM_TM % N_META == 0 and N_META % 8 == 0
    return _final_norm(h.reshape(b, lp, d), row(final_norm), seq)
```

```python
import functools
import math

import jax
import jax.numpy as jnp
from jax import lax
from jax.experimental import pallas as pl
from jax.experimental.pallas import tpu as pltpu

F32 = jnp.float32
BF16 = jnp.bfloat16

N_META = 16
RWKV_HEAD = 64
FOX_HEAD = 128
W_LORA = 64
A_LORA = 64
G_LORA = 160
NORM_EPS = 1e-6
LNX_EPS = 64e-5
L2_EPS = 1e-12
NEG_INF = -1e30
LOG2E = math.log2(math.e)
BIAS_PIECES = 3

LANES = 128
VMEM_LIMIT_BYTES = 56 * 1024 * 1024

CHUNK = 64
RWKV_ROWS = 192
PAIR = 2 * RWKV_HEAD
TIME_ALIGN = 384
ATT_T = 384
FFN_TM = 1024
FFN_TF = 512
NORM_TM = 512
PROJ_TM = 1056
PROJ_TN = 1024
OUT_TM = 512
LORA_W = 512
F_LANE = 448

_NT = (((1,), (1,)), ((), ()))


def _rms_scale(x, gain):
    ms = jnp.mean(x * x, axis=-1, keepdims=True)
    return x * lax.rsqrt(ms + NORM_EPS) * gain


def _dot(a, b):
    return jnp.dot(a.astype(BF16), b.astype(BF16), preferred_element_type=F32)


def _dot_nt(a, b):
    return lax.dot_general(a.astype(BF16), b.astype(BF16), _NT, preferred_element_type=F32)


def _split3(x):
    hi = x.astype(BF16)
    r1 = x - hi.astype(F32)
    mid = r1.astype(BF16)
    lo = (r1 - mid.astype(F32)).astype(BF16)
    return hi, mid, lo


def _dot_exact_lhs(m_bf16, x):
    hi, mid, lo = _split3(x)
    acc = jnp.dot(m_bf16, lo, preferred_element_type=F32)
    acc = acc + jnp.dot(m_bf16, mid, preferred_element_type=F32)
    return acc + jnp.dot(m_bf16, hi, preferred_element_type=F32)


def _params(*sem):
    return pltpu.CompilerParams(dimension_semantics=sem, vmem_limit_bytes=VMEM_LIMIT_BYTES)


def _ffn_body(h_ref, g_ref, wg_ref, wu_ref, wd_ref, o_ref, u_ref):
    f = pl.program_id(1)

    @pl.when(f == 0)
    def _():
        x = h_ref[...]
        u_ref[...] = _rms_scale(x, g_ref[...]).astype(BF16)
        o_ref[...] = x

    u = u_ref[...]
    gate = jnp.dot(u, wg_ref[0], preferred_element_type=F32)
    up = jnp.dot(u, wu_ref[0], preferred_element_type=F32)
    act = (0.5 * gate) * jax.nn.sigmoid(gate) * up
    o_ref[...] += jnp.dot(act.astype(BF16), wd_ref[0], preferred_element_type=F32)


def _ffn(h, gain, w_gu, w_down, layer):
    t, d = h.shape
    ff = w_down.shape[1]
    nf = ff // FFN_TF
    return pl.pallas_call(
        _ffn_body,
        grid=(t // FFN_TM, nf),
        in_specs=[
            pl.BlockSpec((FFN_TM, d), lambda i, f: (i, 0)),
            pl.BlockSpec((1, d), lambda i, f: (0, 0)),
            pl.BlockSpec((1, d, FFN_TF), lambda i, f: (layer, 0, f)),
            pl.BlockSpec((1, d, FFN_TF), lambda i, f: (layer, 0, f + nf)),
            pl.BlockSpec((1, FFN_TF, d), lambda i, f: (layer, f, 0)),
        ],
        out_specs=pl.BlockSpec((FFN_TM, d), lambda i, f: (i, 0)),
        out_shape=jax.ShapeDtypeStruct((t, d), F32),
        scratch_shapes=[pltpu.VMEM((FFN_TM, d), BF16)],
        compiler_params=_params("parallel", "arbitrary"),
    )(h, gain, w_gu, w_gu, w_down)


def _proj_body(h_ref, g_ref, w_ref, wl_ref, o_ref, ol_ref, u_ref, *, scaled_blocks, scale):
    j = pl.program_id(1)

    @pl.when(j == 0)
    def _():
        u_ref[...] = _rms_scale(h_ref[...], g_ref[...]).astype(BF16)

    acc = jnp.dot(u_ref[...], w_ref[...], preferred_element_type=F32)
    o_ref[...] = (acc * jnp.where(j < scaled_blocks, scale, 1.0)).astype(o_ref.dtype)

    @pl.when(j == pl.num_programs(1) - 1)
    def _():
        ol_ref[...] = jnp.dot(u_ref[...], wl_ref[...], preferred_element_type=F32)


def _in_proj(h, gain, w, w_lora, scaled_blocks, scale):
    t, d = h.shape
    n = w.shape[1]
    nl = w_lora.shape[1]
    return pl.pallas_call(
        functools.partial(_proj_body, scaled_blocks=scaled_blocks, scale=scale),
        grid=(t // PROJ_TM, n // PROJ_TN),
        in_specs=[
            pl.BlockSpec((PROJ_TM, d), lambda i, j: (i, 0)),
            pl.BlockSpec((1, d), lambda i, j: (0, 0)),
            pl.BlockSpec((d, PROJ_TN), lambda i, j: (0, j)),
            pl.BlockSpec((d, nl), lambda i, j: (0, 0)),
        ],
        out_specs=[
            pl.BlockSpec((PROJ_TM, PROJ_TN), lambda i, j: (i, j)),
            pl.BlockSpec((PROJ_TM, nl), lambda i, j: (i, 0)),
        ],
        out_shape=[jax.ShapeDtypeStruct((t, n), BF16), jax.ShapeDtypeStruct((t, nl), F32)],
        scratch_shapes=[pltpu.VMEM((PROJ_TM, d), BF16)],
        compiler_params=_params("parallel", "arbitrary"),
    )(h, gain, w, w_lora)


def _forget_bias(x, carry_ref, heads, lane0):
    rows = x.shape[0]
    log_f = jnp.minimum(x, 0.0) - jnp.log1p(jnp.exp(-jnp.abs(x)))
    ci = lax.broadcasted_iota(jnp.int32, (rows, rows), 0)
    cj = lax.broadcasted_iota(jnp.int32, (rows, rows), 1)
    csum = _dot_exact_lhs((cj <= ci).astype(BF16), log_f) + carry_ref[...]
    carry_ref[...] = csum[rows - 1:rows, :]
    pieces = _split3(csum * LOG2E)
    src = lax.broadcasted_iota(jnp.int32, (LANES, LANES), 0)
    dst = lax.broadcasted_iota(jnp.int32, (LANES, LANES), 1)
    acc = jnp.zeros((rows, LANES), F32)
    for i, piece in enumerate(pieces):
        move = ((dst % BIAS_PIECES == i) & (dst < BIAS_PIECES * heads)
                & (src == lane0 + dst // BIAS_PIECES)).astype(BF16)
        acc = acc + jnp.dot(piece, move, preferred_element_type=F32)
    return acc.astype(BF16)


def _rwkv_body(rkv_ref, lora_ref, mu_rkv_ref, mu_lora_ref, w0_ref, wup_ref, a0_ref, aup_ref,
               gup_ref, kk_ref, ka_ref, rk_ref, lnw_ref, lnb_ref, bf_ref, o_ref, cp_ref,
               carry_rkv, carry_lora, st_ref, carry_gate, *, width, fox_heads):
    rows = RWKV_ROWS
    n_ch = rows // CHUNK
    c = pl.program_id(1)

    @pl.when(c == 0)
    def _():
        carry_rkv[...] = jnp.zeros_like(carry_rkv)
        carry_lora[...] = jnp.zeros_like(carry_lora)
        st_ref[...] = jnp.zeros_like(st_ref)
        carry_gate[...] = jnp.zeros_like(carry_gate)

    f_blk = slice(F_LANE // LANES * LANES, (F_LANE // LANES + 1) * LANES)
    cp_ref[0] = _forget_bias(lora_ref[0, :, f_blk] + bf_ref[...], carry_gate, fox_heads, F_LANE % LANES)

    row = lax.broadcasted_iota(jnp.int32, (rows, 1), 0)

    def shift(p, carry_ref, mu):
        prev = jnp.where(row == 0, carry_ref[...], pltpu.roll(p, 1, 0))
        carry_ref[...] = p[rows - 1:rows, :]
        return p + (prev - p) * mu

    pf = shift(rkv_ref[0].astype(F32), carry_rkv, mu_rkv_ref[...])
    lf = shift(lora_ref[0], carry_lora, mu_lora_ref[...])
    r = pf[:, :width]
    k = pf[:, width:2 * width]
    v = pf[:, 2 * width:]
    wd = lf[:, 0:LANES]
    ad = lf[:, LANES:2 * LANES]
    gd = lf[:, 2 * LANES:4 * LANES]

    z = w0_ref[...] + _dot(jnp.tanh(wd), wup_ref[...])
    nz = -z
    softplus = jnp.maximum(nz, 0.0) + jnp.log1p(jnp.exp(-jnp.abs(nz)))
    logw = -jnp.exp(-softplus - 0.5)
    a = jax.nn.sigmoid(a0_ref[...] + _dot(ad, aup_ref[...]))
    g = _dot(jax.nn.sigmoid(gd), gup_ref[...])

    ci = lax.broadcasted_iota(jnp.int32, (rows, rows), 0)
    cj = lax.broadcasted_iota(jnp.int32, (rows, rows), 1)
    ltri = ((cj <= ci) & ((ci // CHUNK) == (cj // CHUNK))).astype(BF16)
    cs = _dot_exact_lhs(ltri, logw)

    ii = lax.broadcasted_iota(jnp.int32, (PAIR, PAIR), 0)
    jj = lax.broadcasted_iota(jnp.int32, (PAIR, PAIR), 1)
    same_head = (ii // RWKV_HEAD) == (jj // RWKV_HEAD)
    mask_strict = same_head & (jj < ii)
    mask_incl = same_head & (jj <= ii)
    eye = ii == jj
    seg_ones = same_head.astype(BF16)
    lane = lax.broadcasted_iota(jnp.int32, (CHUNK, PAIR), 1)
    head0 = lane < RWKV_HEAD

    def stack(x):
        return jnp.concatenate([jnp.where(head0, x, 0.0), jnp.where(head0, 0.0, x)], axis=0)

    def both(x):
        return jnp.concatenate([x, x], axis=0)

    pairs = range(width // PAIR)
    sls = [slice(hp * PAIR, (hp + 1) * PAIR) for hp in pairs]
    chunks = [slice(ch * CHUNK, (ch + 1) * CHUNK) for ch in range(n_ch)]
    units = [(ch, p) for ch in range(n_ch) for p in pairs]

    kk = [k[:, sl] * kk_ref[:, sl] for sl in sls]
    ss = [_dot(x * x, seg_ones) for x in kk]
    kk = [x / jnp.maximum(jnp.sqrt(s), L2_EPS) for x, s in zip(kk, ss)]
    k2 = [k[:, sl] * (1.0 + (a[:, sl] - 1.0) * ka_ref[:, sl]) for sl in sls]
    bv = [x * a[:, sl] for x, sl in zip(kk, sls)]
    e_neg = [jnp.exp(-cs[:, sl]) for sl in sls]
    at_ = [-kk[p] * jnp.exp(cs[:, sls[p]] - logw[:, sls[p]]) for p in pairs]
    rt_ = [r[:, sl] * jnp.exp(cs[:, sl]) for sl in sls]
    bt_ = [bv[p] * e_neg[p] for p in pairs]
    kt_ = [k2[p] * e_neg[p] for p in pairs]

    last = [cs[chunks[ch]][CHUNK - 1:CHUNK, sls[p]] for ch, p in units]
    e_end = [jnp.exp(last[u] - cs[chunks[ch], sls[p]]) for u, (ch, p) in enumerate(units)]
    xa = [stack(at_[p][chunks[ch]]) for ch, p in units]
    xr = [stack(rt_[p][chunks[ch]]) for ch, p in units]
    xbh_t = [stack(bv[p][chunks[ch]] * e_end[u]).T for u, (ch, p) in enumerate(units)]
    xkh_t = [stack(k2[p][chunks[ch]] * e_end[u]).T for u, (ch, p) in enumerate(units)]
    vst = [stack(v[chunks[ch], sls[p]]) for ch, p in units]

    big = [_dot_nt(jnp.concatenate([xa[u], xr[u]], axis=0),
                   jnp.concatenate([both(bt_[p][chunks[ch]]), both(kt_[p][chunks[ch]])], axis=0))
           for u, (ch, p) in enumerate(units)]
    big = [x.astype(BF16) for x in big]
    zero = jnp.zeros((PAIR, PAIR), BF16)

    def packed(mask):
        return mask.astype(BF16) != 0

    m_strict, m_incl = packed(mask_strict), packed(mask_incl)
    a_ab = [x[:PAIR, :PAIR] for x in big]
    a_ak = [jnp.where(m_strict, x[:PAIR, PAIR:], zero) for x in big]
    a_rb = [jnp.where(m_incl, x[PAIR:, :PAIR], zero) for x in big]
    a_rk = [jnp.where(m_incl, x[PAIR:, PAIR:], zero) for x in big]

    lvl1 = packed(((ii ^ jj) == 1) & (jj < ii))
    eye_bf = eye.astype(BF16)
    tinv = [eye_bf + jnp.where(lvl1, x, zero) for x in a_ab]
    s = 2
    while s < RWKV_HEAD:
        e_mask = packed(((ii // (2 * s)) == (jj // (2 * s))) & (((ii // s) % 2) == 1)
                        & (((jj // s) % 2) == 0))
        half = [_dot(t_, jnp.where(e_mask, x, zero)).astype(BF16) for t_, x in zip(tinv, a_ab)]
        tinv = [t_ + _dot(h_, t_).astype(BF16) for t_, h_ in zip(tinv, half)]
        s *= 2

    n_u = range(len(units))
    w_av = [_dot(a_ak[u], vst[u]) for u in n_u]
    x = [_dot(tinv[u], jnp.concatenate([xa[u], w_av[u]], axis=1)) for u in n_u]
    mg = [_dot(xbh_t[u], x[u]) for u in n_u]
    kv = [_dot(xkh_t[u], vst[u]) for u in n_u]
    ry = [_dot(a_rb[u], x[u]) for u in n_u]
    rkv_ = [_dot(a_rk[u], vst[u]) for u in n_u]
    rh = [xr[u] + ry[u][:, :PAIR] for u in n_u]
    y0 = [ry[u][:, PAIR:] + rkv_[u] for u in n_u]
    mm = [mg[u][:, :PAIR] + jnp.where(eye, jnp.exp(last[u]), 0.0) for u in n_u]
    gg = [mg[u][:, PAIR:] + kv[u] for u in n_u]

    st = [st_ref[p] for p in pairs]
    yst = []
    for ch in range(n_ch):
        base = ch * len(pairs)
        yst.append([_dot(rh[base + p], st[p]) + y0[base + p] for p in pairs])
        st = [_dot(mm[base + p], st[p]) + gg[base + p] for p in pairs]
    for p in pairs:
        st_ref[p] = st[p]
    y = [jnp.concatenate([yst[ch][p][:CHUNK] + yst[ch][p][CHUNK:] for ch in range(n_ch)], axis=0)
         for p in pairs]

    inv_n = 1.0 / RWKV_HEAD
    mean = [_dot(x_, seg_ones) * inv_n for x_ in y]
    dlt = [x_ - m_ for x_, m_ in zip(y, mean)]
    var = [_dot(x_ * x_, seg_ones) * inv_n for x_ in dlt]
    bonus = [_dot(r[:, sl] * k2[p] * rk_ref[:, sl], seg_ones) * v[:, sl] for p, sl in zip(pairs, sls)]
    for p, sl in zip(pairs, sls):
        yn = dlt[p] * lax.rsqrt(var[p] + LNX_EPS) * lnw_ref[:, sl] + lnb_ref[:, sl]
        o_ref[0, :, sl] = ((yn + bonus[p]) * g[:, sl]).astype(o_ref.dtype)


def _rwkv(pm3, pl3, prm, width, fox_heads):
    b, lp, _ = pm3.shape
    n_rkv = 3 * width
    vec = lambda n: pl.BlockSpec((1, n), lambda i, c: (0, 0))
    mat = lambda k: pl.BlockSpec((k, width), lambda i, c: (0, 0))
    return pl.pallas_call(
        functools.partial(_rwkv_body, width=width, fox_heads=fox_heads),
        grid=(b, lp // RWKV_ROWS),
        in_specs=[
            pl.BlockSpec((1, RWKV_ROWS, n_rkv), lambda i, c: (i, c, 1)),
            pl.BlockSpec((1, RWKV_ROWS, LORA_W), lambda i, c: (i, c, 0)),
            vec(n_rkv), vec(LORA_W), vec(width), mat(LANES), vec(width), mat(LANES),
            mat(2 * LANES), vec(width), vec(width), vec(width), vec(width), vec(width), vec(LANES),
        ],
        out_specs=[pl.BlockSpec((1, RWKV_ROWS, width), lambda i, c: (i, c, 0)),
                   pl.BlockSpec((1, RWKV_ROWS, LANES), lambda i, c: (i, c, 0))],
        out_shape=[jax.ShapeDtypeStruct((b, lp, width), BF16),
                   jax.ShapeDtypeStruct((b, lp, LANES), BF16)],
        scratch_shapes=[
            pltpu.VMEM((1, n_rkv), F32),
            pltpu.VMEM((1, LORA_W), F32),
            pltpu.VMEM((width // PAIR, PAIR, PAIR), F32),
            pltpu.VMEM((1, LANES), F32),
        ],
        compiler_params=_params("parallel", "arbitrary"),
    )(pm3, pl3, prm["mu_rkv"], prm["mu_lora"], prm["w0"], prm["w_up"], prm["a0"], prm["a_up"],
      prm["g_up"], prm["k_k"], prm["k_a"], prm["r_k"], prm["lnx_w"], prm["lnx_b"], prm["b_f"])


def _attn_body(q_ref, k_ref, vt_ref, cp_ref, o_ref, m_ref, l_ref, acc_ref, *, heads):
    qi = pl.program_id(1)
    m_ref[...] = jnp.full_like(m_ref, NEG_INF)
    l_ref[...] = jnp.zeros_like(l_ref)
    acc_ref[...] = jnp.zeros_like(acc_ref)
    hs = [slice(g * FOX_HEAD, (g + 1) * FOX_HEAD) for g in range(heads)]
    gs = range(heads)
    lane = lax.broadcasted_iota(jnp.int32, (ATT_T, LANES), 1)
    q_aug = [jnp.concatenate(
        [q_ref[0, :, hs[g]], -(lane // BIAS_PIECES == g).astype(BF16)], axis=1) for g in gs]

    def block(j, masked):
        off = pl.multiple_of(j * ATT_T, ATT_T)
        bias = cp_ref[0, pl.ds(off, ATT_T), :]
        s = [lax.dot_general(
            jnp.concatenate([k_ref[0, pl.ds(off, ATT_T), hs[g]], bias], axis=1),
            q_aug[g], _NT, preferred_element_type=F32) for g in gs]
        if masked:
            kpos = lax.broadcasted_iota(jnp.int32, (ATT_T, ATT_T), 0)
            qpos = lax.broadcasted_iota(jnp.int32, (ATT_T, ATT_T), 1)
            s = [jnp.where(kpos <= qpos, x, NEG_INF) for x in s]
        m_prev = [m_ref[g] for g in gs]
        m_new = [jnp.maximum(m_prev[g], jnp.max(s[g], axis=0, keepdims=True)) for g in gs]
        p = [jnp.exp2(s[g] - m_new[g]) for g in gs]
        alpha = [jnp.exp2(m_prev[g] - m_new[g]) for g in gs]
        pv = [jnp.dot(vt_ref[0, hs[g], pl.ds(off, ATT_T)], p[g].astype(BF16),
                      preferred_element_type=F32) for g in gs]
        for g in gs:
            l_ref[g] = alpha[g] * l_ref[g] + jnp.sum(p[g], axis=0, keepdims=True)
            acc_ref[hs[g], :] = alpha[g] * acc_ref[hs[g], :] + pv[g]
            m_ref[g] = m_new[g]

    def body(j, carry):
        block(j, False)
        return carry

    lax.fori_loop(0, qi, body, 0)
    block(qi, True)
    for g in gs:
        o_ref[0, :, hs[g]] = (acc_ref[hs[g], :] / l_ref[g]).T


def _attention(qkv3, vt3, cp, width):
    b, lp, _ = qkv3.shape
    heads = width // FOX_HEAD
    resident = pl.Buffered(1)
    return pl.pallas_call(
        functools.partial(_attn_body, heads=heads),
        grid=(b, lp // ATT_T),
        in_specs=[
            pl.BlockSpec((1, ATT_T, width), lambda i, q: (i, q, 0)),
            pl.BlockSpec((1, lp, width), lambda i, q: (i, 0, 1), pipeline_mode=resident),
            pl.BlockSpec((1, width, lp), lambda i, q: (i, 0, 0), pipeline_mode=resident),
            pl.BlockSpec((1, lp, LANES), lambda i, q: (i, 0, 0), pipeline_mode=resident),
        ],
        out_specs=pl.BlockSpec((1, ATT_T, width), lambda i, q: (i, q, 0)),
        out_shape=jax.ShapeDtypeStruct((b, lp, width), F32),
        scratch_shapes=[
            pltpu.VMEM((heads, 1, ATT_T), F32),
            pltpu.VMEM((heads, 1, ATT_T), F32),
            pltpu.VMEM((width, ATT_T), F32),
        ],
        compiler_params=_params("parallel", "arbitrary"),
    )(qkv3, qkv3, vt3, cp)


def _out_body(h_ref, yr_ref, yf_ref, gn_ref, w1_ref, w2_ref, o_ref):
    yf = _rms_scale(yf_ref[...], gn_ref[...])
    acc = jnp.dot(yr_ref[...], w1_ref[0], preferred_element_type=F32)
    acc = acc + jnp.dot(yf.astype(BF16), w2_ref[0], preferred_element_type=F32)
    o_ref[...] = h_ref[...] + acc


def _out_proj(h, y_rwkv, y_fox, fox_gain, w_out, layer):
    t, d = h.shape
    w1 = y_rwkv.shape[1]
    w2 = y_fox.shape[1]
    return pl.pallas_call(
        _out_body,
        grid=(t // OUT_TM,),
        in_specs=[
            pl.BlockSpec((OUT_TM, d), lambda i: (i, 0)),
            pl.BlockSpec((OUT_TM, w1), lambda i: (i, 0)),
            pl.BlockSpec((OUT_TM, w2), lambda i: (i, 0)),
            pl.BlockSpec((1, w2), lambda i: (0, 0)),
            pl.BlockSpec((1, w1, d), lambda i: (layer, 0, 0)),
            pl.BlockSpec((1, w2, d), lambda i: (layer, 1, 0)),
        ],
        out_specs=pl.BlockSpec((OUT_TM, d), lambda i: (i, 0)),
        out_shape=jax.ShapeDtypeStruct((t, d), F32),
        compiler_params=_params("parallel"),
    )(h, y_rwkv, y_fox, fox_gain, w_out, w_out)


def _norm_body(a_ref, b_ref, g_ref, o_ref):
    x = jnp.concatenate([a_ref[0, N_META:, :], b_ref[0]], axis=0)
    o_ref[0] = _rms_scale(x, g_ref[...])


def _final_norm(h3, gain, seq):
    b, _, d = h3.shape
    per = NORM_TM // N_META
    return pl.pallas_call(
        _norm_body,
        grid=(b, seq // NORM_TM),
        in_specs=[
            pl.BlockSpec((1, NORM_TM, d), lambda i, r: (i, r, 0)),
            pl.BlockSpec((1, N_META, d), lambda i, r: (i, (r + 1) * per, 0)),
            pl.BlockSpec((1, d), lambda i, r: (0, 0)),
        ],
        out_specs=pl.BlockSpec((1, NORM_TM, d), lambda i, r: (i, r, 0)),
        out_shape=jax.ShapeDtypeStruct((b, seq, d), F32),
        compiler_params=_params("parallel", "parallel"),
    )(h3, h3, gain)


def _pad_rows(w, rows):
    return jnp.pad(w, ((0, rows - w.shape[0]), (0, 0)))


def _pack_w_in(w_in, width, heads):
    d = w_in.shape[0]
    c3 = 3 * width
    c4, c5, c6 = c3 + W_LORA, c3 + W_LORA + A_LORA, c3 + W_LORA + A_LORA + G_LORA
    lora = jnp.zeros((d, LORA_W), w_in.dtype)
    lora = lora.at[:, 0:W_LORA].set(w_in[:, c3:c4])
    lora = lora.at[:, LANES:LANES + A_LORA].set(w_in[:, c4:c5])
    lora = lora.at[:, 2 * LANES:2 * LANES + G_LORA].set(w_in[:, c5:c6])
    lora = lora.at[:, F_LANE:F_LANE + heads].set(w_in[:, c6 + c3:c6 + c3 + heads])
    return jnp.concatenate([w_in[:, c6:c6 + c3], w_in[:, :c3]], axis=1).astype(BF16), lora.astype(BF16)


def _pack_mu_lora(mu, width):
    c3 = 3 * width
    c4, c5, c6 = c3 + W_LORA, c3 + W_LORA + A_LORA, c3 + W_LORA + A_LORA + G_LORA
    out = jnp.zeros((LORA_W,), mu.dtype)
    out = out.at[0:W_LORA].set(mu[c3:c4])
    out = out.at[LANES:LANES + A_LORA].set(mu[c4:c5])
    out = out.at[2 * LANES:2 * LANES + G_LORA].set(mu[c5:c6])
    return out[None, :]


def kernel(x, meta_tokens, ffn1_norm, ffn1_w_gu, ffn1_w_down, mix_norm, w_in, rwkv_mu, rwkv_w0,
           rwkv_w_up, rwkv_a0, rwkv_a_up, rwkv_g_up, rwkv_k_k, rwkv_k_a, rwkv_r_k, rwkv_lnx_w,
           rwkv_lnx_b, fox_b_f, fox_out_norm, w_out, ffn2_norm, ffn2_w_gu, ffn2_w_down, final_norm):
    b, seq, d = x.shape
    depth = w_in.shape[0]
    width = rwkv_w0.shape[1]
    heads = fox_b_f.shape[1]
    assert rwkv_g_up.shape[1] == G_LORA and G_LORA <= 2 * LANES
    assert F_LANE >= 2 * LANES + G_LORA and F_LANE + heads <= LORA_W and F_LANE % LANES + heads <= LANES
    l = N_META + seq
    lp = -(-l // TIME_ALIGN) * TIME_ALIGN
    t = b * lp
    assert t % PROJ_TM == 0 and t % FFN_TM == 0 and lp % RWKV_ROWS == 0 and RWKV_ROWS % CHUNK == 0
    assert width % PROJ_TN == 0 and BIAS_PIECES * heads <= LANES

    meta = jnp.broadcast_to(meta_tokens.astype(x.dtype)[None], (b, N_META, d))
    h = jnp.concatenate([meta, x, jnp.zeros((b, lp - l, d), x.dtype)], axis=1).reshape(t, d)

    row = lambda vct: vct.astype(F32)[None, :]
    f_lane0 = F_LANE % LANES
    wgu1, wdn1 = ffn1_w_gu.astype(BF16), ffn1_w_down.astype(BF16)
    wgu2, wdn2 = ffn2_w_gu.astype(BF16), ffn2_w_down.astype(BF16)
    w_out_bf = w_out.astype(BF16)
    for i in range(depth):
        h = _ffn(h, row(ffn1_norm[i]), wgu1, wdn1, i)
        p_main, p_lora = _in_proj(h, row(mix_norm[i]), *_pack_w_in(w_in[i], width, heads),
                                  scaled_blocks=width // PROJ_TN, scale=FOX_HEAD ** -0.5 * LOG2E)
        pm3 = p_main.reshape(b, lp, 6 * width)
        pl3 = p_lora.reshape(b, lp, LORA_W)
        prm = dict(
            mu_rkv=row(rwkv_mu[i, :3 * width]), mu_lora=_pack_mu_lora(rwkv_mu[i], width),
            w0=row(rwkv_w0[i]), w_up=_pad_rows(rwkv_w_up[i], LANES).astype(BF16),
            a0=row(rwkv_a0[i]), a_up=_pad_rows(rwkv_a_up[i], LANES).astype(BF16),
            g_up=_pad_rows(rwkv_g_up[i], 2 * LANES).astype(BF16),
            k_k=row(rwkv_k_k[i]), k_a=row(rwkv_k_a[i]), r_k=row(rwkv_r_k[i].reshape(-1)),
            lnx_w=row(rwkv_lnx_w[i]), lnx_b=row(rwkv_lnx_b[i]),
            b_f=jnp.zeros((1, LANES), F32).at[0, f_lane0:f_lane0 + heads].set(fox_b_f[i]))
        y_rwkv, cp = _rwkv(pm3, pl3, prm, width, heads)
        vt3 = jnp.transpose(pm3[:, :, 2 * width:3 * width], (0, 2, 1))
        y_fox = _attention(pm3, vt3, cp, width)
        h = _out_proj(h, y_rwkv.reshape(t, width), y_fox.reshape(t, width), row(fox_out_norm[i]),
                      w_out_bf, i)
        h = _ffn(h, row(ffn2_norm[i]), wgu2, wdn2, i)
    assert seq % NORM_TM == 0 and NORM_TM % N_META == 0 and N_META % 8 == 0
    return _final_norm(h.reshape(b, lp, d), row(final_norm), seq)
```

```python
import functools
import math

import jax
import jax.numpy as jnp
from jax import lax
from jax.experimental import pallas as pl
from jax.experimental.pallas import tpu as pltpu

F32 = jnp.float32
BF16 = jnp.bfloat16

N_META = 16
RWKV_HEAD = 64
FOX_HEAD = 128
W_LORA = 64
A_LORA = 64
G_LORA = 160
NORM_EPS = 1e-6
LNX_EPS = 64e-5
L2_EPS = 1e-12
NEG_INF = -1e30
LOG2E = math.log2(math.e)
BIAS_PIECES = 3

LANES = 128
VMEM_LIMIT_BYTES = 56 * 1024 * 1024

CHUNK = 64
RWKV_ROWS = 192
PAIR = 2 * RWKV_HEAD
TIME_ALIGN = 384
ATT_T = 384
FFN_TM = 1024
FFN_TF = 512
NORM_TM = 512
PROJ_TM = 1056
PROJ_TN = 1024
OUT_TM = 512
LORA_W = 512
F_LANE = 448

_NT = (((1,), (1,)), ((), ()))


def _rms_scale(x, gain):
    ms = jnp.mean(x * x, axis=-1, keepdims=True)
    return x * lax.rsqrt(ms + NORM_EPS) * gain


def _dot(a, b):
    return jnp.dot(a.astype(BF16), b.astype(BF16), preferred_element_type=F32)


def _dot_nt(a, b):
    return lax.dot_general(a.astype(BF16), b.astype(BF16), _NT, preferred_element_type=F32)


def _split3(x):
    hi = x.astype(BF16)
    r1 = x - hi.astype(F32)
    mid = r1.astype(BF16)
    lo = (r1 - mid.astype(F32)).astype(BF16)
    return hi, mid, lo


def _dot_exact_lhs(m_bf16, x):
    hi, mid, lo = _split3(x)
    acc = jnp.dot(m_bf16, lo, preferred_element_type=F32)
    acc = acc + jnp.dot(m_bf16, mid, preferred_element_type=F32)
    return acc + jnp.dot(m_bf16, hi, preferred_element_type=F32)


def _params(*sem):
    return pltpu.CompilerParams(dimension_semantics=sem, vmem_limit_bytes=VMEM_LIMIT_BYTES)


def _ffn_body(h_ref, g_ref, wg_ref, wu_ref, wd_ref, o_ref, u_ref):
    f = pl.program_id(1)

    @pl.when(f == 0)
    def _():
        x = h_ref[...]
        u_ref[...] = _rms_scale(x, g_ref[...]).astype(BF16)
        o_ref[...] = x

    u = u_ref[...]
    gate = jnp.dot(u, wg_ref[0], preferred_element_type=F32)
    up = jnp.dot(u, wu_ref[0], preferred_element_type=F32)
    act = (0.5 * gate) * jax.nn.sigmoid(gate) * up
    o_ref[...] += jnp.dot(act.astype(BF16), wd_ref[0], preferred_element_type=F32)


def _ffn(h, gain, w_gu, w_down, layer):
    t, d = h.shape
    ff = w_down.shape[1]
    nf = ff // FFN_TF
    return pl.pallas_call(
        _ffn_body,
        grid=(t // FFN_TM, nf),
        in_specs=[
            pl.BlockSpec((FFN_TM, d), lambda i, f: (i, 0)),
            pl.BlockSpec((1, d), lambda i, f: (0, 0)),
            pl.BlockSpec((1, d, FFN_TF), lambda i, f: (layer, 0, f)),
            pl.BlockSpec((1, d, FFN_TF), lambda i, f: (layer, 0, f + nf)),
            pl.BlockSpec((1, FFN_TF, d), lambda i, f: (layer, f, 0)),
        ],
        out_specs=pl.BlockSpec((FFN_TM, d), lambda i, f: (i, 0)),
        out_shape=jax.ShapeDtypeStruct((t, d), F32),
        scratch_shapes=[pltpu.VMEM((FFN_TM, d), BF16)],
        compiler_params=_params("parallel", "arbitrary"),
    )(h, gain, w_gu, w_gu, w_down)


def _proj_body(h_ref, g_ref, w_ref, wl_ref, o_ref, ol_ref, u_ref, *, scaled_blocks, scale):
    j = pl.program_id(1)

    @pl.when(j == 0)
    def _():
        u_ref[...] = _rms_scale(h_ref[...], g_ref[...]).astype(BF16)

    acc = jnp.dot(u_ref[...], w_ref[...], preferred_element_type=F32)
    o_ref[...] = (acc * jnp.where(j < scaled_blocks, scale, 1.0)).astype(o_ref.dtype)

    @pl.when(j == pl.num_programs(1) - 1)
    def _():
        ol_ref[...] = jnp.dot(u_ref[...], wl_ref[...], preferred_element_type=F32)


def _in_proj(h, gain, w, w_lora, scaled_blocks, scale):
    t, d = h.shape
    n = w.shape[1]
    nl = w_lora.shape[1]
    return pl.pallas_call(
        functools.partial(_proj_body, scaled_blocks=scaled_blocks, scale=scale),
        grid=(t // PROJ_TM, n // PROJ_TN),
        in_specs=[
            pl.BlockSpec((PROJ_TM, d), lambda i, j: (i, 0)),
            pl.BlockSpec((1, d), lambda i, j: (0, 0)),
            pl.BlockSpec((d, PROJ_TN), lambda i, j: (0, j)),
            pl.BlockSpec((d, nl), lambda i, j: (0, 0)),
        ],
        out_specs=[
            pl.BlockSpec((PROJ_TM, PROJ_TN), lambda i, j: (i, j)),
            pl.BlockSpec((PROJ_TM, nl), lambda i, j: (i, 0)),
        ],
        out_shape=[jax.ShapeDtypeStruct((t, n), BF16), jax.ShapeDtypeStruct((t, nl), F32)],
        scratch_shapes=[pltpu.VMEM((PROJ_TM, d), BF16)],
        compiler_params=_params("parallel", "arbitrary"),
    )(h, gain, w, w_lora)


def _forget_bias(x, carry_ref, heads, lane0):
    rows = x.shape[0]
    log_f = jnp.minimum(x, 0.0) - jnp.log1p(jnp.exp(-jnp.abs(x)))
    ci = lax.broadcasted_iota(jnp.int32, (rows, rows), 0)
    cj = lax.broadcasted_iota(jnp.int32, (rows, rows), 1)
    csum = _dot_exact_lhs((cj <= ci).astype(BF16), log_f) + carry_ref[...]
    carry_ref[...] = csum[rows - 1:rows, :]
    pieces = _split3(csum * LOG2E)
    src = lax.broadcasted_iota(jnp.int32, (LANES, LANES), 0)
    dst = lax.broadcasted_iota(jnp.int32, (LANES, LANES), 1)
    acc = jnp.zeros((rows, LANES), F32)
    for i, piece in enumerate(pieces):
        move = ((dst % BIAS_PIECES == i) & (dst < BIAS_PIECES * heads)
                & (src == lane0 + dst // BIAS_PIECES)).astype(BF16)
        acc = acc + jnp.dot(piece, move, preferred_element_type=F32)
    return acc.astype(BF16)


def _rwkv_body(rkv_ref, lora_ref, mu_rkv_ref, mu_lora_ref, w0_ref, wup_ref, a0_ref, aup_ref,
               gup_ref, kk_ref, ka_ref, rk_ref, lnw_ref, lnb_ref, bf_ref, o_ref, cp_ref,
               carry_rkv, carry_lora, st_ref, carry_gate, *, width, fox_heads):
    rows = RWKV_ROWS
    n_ch = rows // CHUNK
    c = pl.program_id(1)

    @pl.when(c == 0)
    def _():
        carry_rkv[...] = jnp.zeros_like(carry_rkv)
        carry_lora[...] = jnp.zeros_like(carry_lora)
        st_ref[...] = jnp.zeros_like(st_ref)
        carry_gate[...] = jnp.zeros_like(carry_gate)

    f_blk = slice(F_LANE // LANES * LANES, (F_LANE // LANES + 1) * LANES)
    cp_ref[0] = _forget_bias(lora_ref[0, :, f_blk] + bf_ref[...], carry_gate, fox_heads, F_LANE % LANES)

    row = lax.broadcasted_iota(jnp.int32, (rows, 1), 0)

    def shift(p, carry_ref, mu):
        prev = jnp.where(row == 0, carry_ref[...], pltpu.roll(p, 1, 0))
        carry_ref[...] = p[rows - 1:rows, :]
        return p + (prev - p) * mu

    pf = shift(rkv_ref[0].astype(F32), carry_rkv, mu_rkv_ref[...])
    lf = shift(lora_ref[0], carry_lora, mu_lora_ref[...])
    r = pf[:, :width]
    k = pf[:, width:2 * width]
    v = pf[:, 2 * width:]
    wd = lf[:, 0:LANES]
    ad = lf[:, LANES:2 * LANES]
    gd = lf[:, 2 * LANES:4 * LANES]

    z = w0_ref[...] + _dot(jnp.tanh(wd), wup_ref[...])
    nz = -z
    softplus = jnp.maximum(nz, 0.0) + jnp.log1p(jnp.exp(-jnp.abs(nz)))
    logw = -jnp.exp(-softplus - 0.5)
    a = jax.nn.sigmoid(a0_ref[...] + _dot(ad, aup_ref[...]))
    g = _dot(jax.nn.sigmoid(gd), gup_ref[...])

    ci = lax.broadcasted_iota(jnp.int32, (rows, rows), 0)
    cj = lax.broadcasted_iota(jnp.int32, (rows, rows), 1)
    ltri = ((cj <= ci) & ((ci // CHUNK) == (cj // CHUNK))).astype(BF16)
    cs = _dot_exact_lhs(ltri, logw)

    ii = lax.broadcasted_iota(jnp.int32, (PAIR, PAIR), 0)
    jj = lax.broadcasted_iota(jnp.int32, (PAIR, PAIR), 1)
    same_head = (ii // RWKV_HEAD) == (jj // RWKV_HEAD)
    mask_strict = same_head & (jj < ii)
    mask_incl = same_head & (jj <= ii)
    eye = ii == jj
    seg_ones = same_head.astype(BF16)
    lane = lax.broadcasted_iota(jnp.int32, (CHUNK, PAIR), 1)
    head0 = lane < RWKV_HEAD

    def stack(x):
        return jnp.concatenate([jnp.where(head0, x, 0.0), jnp.where(head0, 0.0, x)], axis=0)

    def both(x):
        return jnp.concatenate([x, x], axis=0)

    pairs = range(width // PAIR)
    sls = [slice(hp * PAIR, (hp + 1) * PAIR) for hp in pairs]
    chunks = [slice(ch * CHUNK, (ch + 1) * CHUNK) for ch in range(n_ch)]
    units = [(ch, p) for ch in range(n_ch) for p in pairs]

    kk = [k[:, sl] * kk_ref[:, sl] for sl in sls]
    ss = [_dot(x * x, seg_ones) for x in kk]
    kk = [x / jnp.maximum(jnp.sqrt(s), L2_EPS) for x, s in zip(kk, ss)]
    k2 = [k[:, sl] * (1.0 + (a[:, sl] - 1.0) * ka_ref[:, sl]) for sl in sls]
    bv = [x * a[:, sl] for x, sl in zip(kk, sls)]
    e_neg = [jnp.exp(-cs[:, sl]) for sl in sls]
    at_ = [-kk[p] * jnp.exp(cs[:, sls[p]] - logw[:, sls[p]]) for p in pairs]
    rt_ = [r[:, sl] * jnp.exp(cs[:, sl]) for sl in sls]
    bt_ = [bv[p] * e_neg[p] for p in pairs]
    kt_ = [k2[p] * e_neg[p] for p in pairs]

    last = [cs[chunks[ch]][CHUNK - 1:CHUNK, sls[p]] for ch, p in units]
    e_end = [jnp.exp(last[u] - cs[chunks[ch], sls[p]]) for u, (ch, p) in enumerate(units)]
    xa = [stack(at_[p][chunks[ch]]) for ch, p in units]
    xr = [stack(rt_[p][chunks[ch]]) for ch, p in units]
    xbh_t = [stack(bv[p][chunks[ch]] * e_end[u]).T for u, (ch, p) in enumerate(units)]
    xkh_t = [stack(k2[p][chunks[ch]] * e_end[u]).T for u, (ch, p) in enumerate(units)]
    vst = [stack(v[chunks[ch], sls[p]]) for ch, p in units]

    big = [_dot_nt(jnp.concatenate([xa[u], xr[u]], axis=0),
                   jnp.concatenate([both(bt_[p][chunks[ch]]), both(kt_[p][chunks[ch]])], axis=0))
           for u, (ch, p) in enumerate(units)]
    big = [x.astype(BF16) for x in big]
    zero = jnp.zeros((PAIR, PAIR), BF16)

    def packed(mask):
        return mask.astype(BF16) != 0

    m_strict, m_incl = packed(mask_strict), packed(mask_incl)
    a_ab = [x[:PAIR, :PAIR] for x in big]
    a_ak = [jnp.where(m_strict, x[:PAIR, PAIR:], zero) for x in big]
    a_rb = [jnp.where(m_incl, x[PAIR:, :PAIR], zero) for x in big]
    a_rk = [jnp.where(m_incl, x[PAIR:, PAIR:], zero) for x in big]

    lvl1 = packed(((ii ^ jj) == 1) & (jj < ii))
    eye_bf = eye.astype(BF16)
    tinv = [eye_bf + jnp.where(lvl1, x, zero) for x in a_ab]
    s = 2
    while s < RWKV_HEAD:
        e_mask = packed(((ii // (2 * s)) == (jj // (2 * s))) & (((ii // s) % 2) == 1)
                        & (((jj // s) % 2) == 0))
        half = [_dot(t_, jnp.where(e_mask, x, zero)).astype(BF16) for t_, x in zip(tinv, a_ab)]
        tinv = [t_ + _dot(h_, t_).astype(BF16) for t_, h_ in zip(tinv, half)]
        s *= 2

    n_u = range(len(units))
    w_av = [_dot(a_ak[u], vst[u]) for u in n_u]
    x = [_dot(tinv[u], jnp.concatenate([xa[u], w_av[u]], axis=1)) for u in n_u]
    a_hat = [x_[:, :PAIR].astype(BF16) for x_ in x]
    u0_v = [jnp.concatenate([x[u][:, PAIR:].astype(BF16), vst[u].astype(BF16)], axis=0) for u in n_u]
    mm = [_dot(xbh_t[u], a_hat[u]) + jnp.where(eye, jnp.exp(last[u]), 0.0) for u in n_u]
    gg = [_dot(jnp.concatenate([xbh_t[u], xkh_t[u]], axis=1), u0_v[u]) for u in n_u]
    rh = [xr[u] + _dot(a_rb[u], a_hat[u]) for u in n_u]
    y0 = [_dot(jnp.concatenate([a_rb[u], a_rk[u]], axis=1), u0_v[u]) for u in n_u]

    st = [st_ref[p] for p in pairs]
    yst = []
    for ch in range(n_ch):
        base = ch * len(pairs)
        yst.append([_dot(rh[base + p], st[p]) + y0[base + p] for p in pairs])
        st = [_dot(mm[base + p], st[p]) + gg[base + p] for p in pairs]
    for p in pairs:
        st_ref[p] = st[p]
    y = [jnp.concatenate([yst[ch][p][:CHUNK] + yst[ch][p][CHUNK:] for ch in range(n_ch)], axis=0)
         for p in pairs]

    inv_n = 1.0 / RWKV_HEAD
    mean = [_dot(x_, seg_ones) * inv_n for x_ in y]
    dlt = [x_ - m_ for x_, m_ in zip(y, mean)]
    var = [_dot(x_ * x_, seg_ones) * inv_n for x_ in dlt]
    bonus = [_dot(r[:, sl] * k2[p] * rk_ref[:, sl], seg_ones) * v[:, sl] for p, sl in zip(pairs, sls)]
    for p, sl in zip(pairs, sls):
        yn = dlt[p] * lax.rsqrt(var[p] + LNX_EPS) * lnw_ref[:, sl] + lnb_ref[:, sl]
        o_ref[0, :, sl] = ((yn + bonus[p]) * g[:, sl]).astype(o_ref.dtype)


def _rwkv(pm3, pl3, prm, width, fox_heads):
    b, lp, _ = pm3.shape
    n_rkv = 3 * width
    vec = lambda n: pl.BlockSpec((1, n), lambda i, c: (0, 0))
    mat = lambda k: pl.BlockSpec((k, width), lambda i, c: (0, 0))
    return pl.pallas_call(
        functools.partial(_rwkv_body, width=width, fox_heads=fox_heads),
        grid=(b, lp // RWKV_ROWS),
        in_specs=[
            pl.BlockSpec((1, RWKV_ROWS, n_rkv), lambda i, c: (i, c, 1)),
            pl.BlockSpec((1, RWKV_ROWS, LORA_W), lambda i, c: (i, c, 0)),
            vec(n_rkv), vec(LORA_W), vec(width), mat(LANES), vec(width), mat(LANES),
            mat(2 * LANES), vec(width), vec(width), vec(width), vec(width), vec(width), vec(LANES),
        ],
        out_specs=[pl.BlockSpec((1, RWKV_ROWS, width), lambda i, c: (i, c, 0)),
                   pl.BlockSpec((1, RWKV_ROWS, LANES), lambda i, c: (i, c, 0))],
        out_shape=[jax.ShapeDtypeStruct((b, lp, width), BF16),
                   jax.ShapeDtypeStruct((b, lp, LANES), BF16)],
        scratch_shapes=[
            pltpu.VMEM((1, n_rkv), F32),
            pltpu.VMEM((1, LORA_W), F32),
            pltpu.VMEM((width // PAIR, PAIR, PAIR), F32),
            pltpu.VMEM((1, LANES), F32),
        ],
        compiler_params=_params("parallel", "arbitrary"),
    )(pm3, pl3, prm["mu_rkv"], prm["mu_lora"], prm["w0"], prm["w_up"], prm["a0"], prm["a_up"],
      prm["g_up"], prm["k_k"], prm["k_a"], prm["r_k"], prm["lnx_w"], prm["lnx_b"], prm["b_f"])


def _attn_body(q_ref, k_ref, vt_ref, cp_ref, o_ref, m_ref, l_ref, acc_ref, *, heads):
    qi = pl.program_id(1)
    m_ref[...] = jnp.full_like(m_ref, NEG_INF)
    l_ref[...] = jnp.zeros_like(l_ref)
    acc_ref[...] = jnp.zeros_like(acc_ref)
    hs = [slice(g * FOX_HEAD, (g + 1) * FOX_HEAD) for g in range(heads)]
    gs = range(heads)
    lane = lax.broadcasted_iota(jnp.int32, (ATT_T, LANES), 1)
    q_aug = [jnp.concatenate(
        [q_ref[0, :, hs[g]], -(lane // BIAS_PIECES == g).astype(BF16)], axis=1) for g in gs]

    def block(j, masked):
        off = pl.multiple_of(j * ATT_T, ATT_T)
        bias = cp_ref[0, pl.ds(off, ATT_T), :]
        s = [lax.dot_general(
            jnp.concatenate([k_ref[0, pl.ds(off, ATT_T), hs[g]], bias], axis=1),
            q_aug[g], _NT, preferred_element_type=F32) for g in gs]
        if masked:
            kpos = lax.broadcasted_iota(jnp.int32, (ATT_T, ATT_T), 0)
            qpos = lax.broadcasted_iota(jnp.int32, (ATT_T, ATT_T), 1)
            s = [jnp.where(kpos <= qpos, x, NEG_INF) for x in s]
        m_prev = [m_ref[g] for g in gs]
        m_new = [jnp.maximum(m_prev[g], jnp.max(s[g], axis=0, keepdims=True)) for g in gs]
        p = [jnp.exp2(s[g] - m_new[g]) for g in gs]
        alpha = [jnp.exp2(m_prev[g] - m_new[g]) for g in gs]
        pv = [jnp.dot(vt_ref[0, hs[g], pl.ds(off, ATT_T)], p[g].astype(BF16),
                      preferred_element_type=F32) for g in gs]
        for g in gs:
            l_ref[g] = alpha[g] * l_ref[g] + jnp.sum(p[g], axis=0, keepdims=True)
            acc_ref[hs[g], :] = alpha[g] * acc_ref[hs[g], :] + pv[g]
            m_ref[g] = m_new[g]

    def body(t, carry):
        block(2 * t, False)
        block(2 * t + 1, False)
        return carry

    lax.fori_loop(0, qi // 2, body, 0)

    @pl.when(qi % 2 == 1)
    def _():
        block(qi - 1, False)

    block(qi, True)
    for g in gs:
        o_ref[0, :, hs[g]] = (acc_ref[hs[g], :] / l_ref[g]).T


def _attention(qkv3, vt3, cp, width):
    b, lp, _ = qkv3.shape
    heads = width // FOX_HEAD
    resident = pl.Buffered(1)
    return pl.pallas_call(
        functools.partial(_attn_body, heads=heads),
        grid=(b, lp // ATT_T),
        in_specs=[
            pl.BlockSpec((1, ATT_T, width), lambda i, q: (i, q, 0)),
            pl.BlockSpec((1, lp, width), lambda i, q: (i, 0, 1)),
            pl.BlockSpec((1, width, lp), lambda i, q: (i, 0, 0), pipeline_mode=resident),
            pl.BlockSpec((1, lp, LANES), lambda i, q: (i, 0, 0), pipeline_mode=resident),
        ],
        out_specs=pl.BlockSpec((1, ATT_T, width), lambda i, q: (i, q, 0)),
        out_shape=jax.ShapeDtypeStruct((b, lp, width), F32),
        scratch_shapes=[
            pltpu.VMEM((heads, 1, ATT_T), F32),
            pltpu.VMEM((heads, 1, ATT_T), F32),
            pltpu.VMEM((width, ATT_T), F32),
        ],
        compiler_params=_params("parallel", "arbitrary"),
    )(qkv3, qkv3, vt3, cp)


def _out_body(h_ref, yr_ref, yf_ref, gn_ref, w1_ref, w2_ref, o_ref):
    yf = _rms_scale(yf_ref[...], gn_ref[...])
    acc = jnp.dot(yr_ref[...], w1_ref[0], preferred_element_type=F32)
    acc = acc + jnp.dot(yf.astype(BF16), w2_ref[0], preferred_element_type=F32)
    o_ref[...] = h_ref[...] + acc


def _out_proj(h, y_rwkv, y_fox, fox_gain, w_out, layer):
    t, d = h.shape
    w1 = y_rwkv.shape[1]
    w2 = y_fox.shape[1]
    return pl.pallas_call(
        _out_body,
        grid=(t // OUT_TM,),
        in_specs=[
            pl.BlockSpec((OUT_TM, d), lambda i: (i, 0)),
            pl.BlockSpec((OUT_TM, w1), lambda i: (i, 0)),
            pl.BlockSpec((OUT_TM, w2), lambda i: (i, 0)),
            pl.BlockSpec((1, w2), lambda i: (0, 0)),
            pl.BlockSpec((1, w1, d), lambda i: (layer, 0, 0)),
            pl.BlockSpec((1, w2, d), lambda i: (layer, 1, 0)),
        ],
        out_specs=pl.BlockSpec((OUT_TM, d), lambda i: (i, 0)),
        out_shape=jax.ShapeDtypeStruct((t, d), F32),
        compiler_params=_params("parallel"),
    )(h, y_rwkv, y_fox, fox_gain, w_out, w_out)


def _norm_body(a_ref, b_ref, g_ref, o_ref):
    x = jnp.concatenate([a_ref[0, N_META:, :], b_ref[0]], axis=0)
    o_ref[0] = _rms_scale(x, g_ref[...])


def _final_norm(h3, gain, seq):
    b, _, d = h3.shape
    per = NORM_TM // N_META
    return pl.pallas_call(
        _norm_body,
        grid=(b, seq // NORM_TM),
        in_specs=[
            pl.BlockSpec((1, NORM_TM, d), lambda i, r: (i, r, 0)),
            pl.BlockSpec((1, N_META, d), lambda i, r: (i, (r + 1) * per, 0)),
            pl.BlockSpec((1, d), lambda i, r: (0, 0)),
        ],
        out_specs=pl.BlockSpec((1, NORM_TM, d), lambda i, r: (i, r, 0)),
        out_shape=jax.ShapeDtypeStruct((b, seq, d), F32),
        compiler_params=_params("parallel", "parallel"),
    )(h3, h3, gain)


def _pad_rows(w, rows):
    return jnp.pad(w, ((0, rows - w.shape[0]), (0, 0)))


def _pack_w_in(w_in, width, heads):
    d = w_in.shape[0]
    c3 = 3 * width
    c4, c5, c6 = c3 + W_LORA, c3 + W_LORA + A_LORA, c3 + W_LORA + A_LORA + G_LORA
    lora = jnp.zeros((d, LORA_W), w_in.dtype)
    lora = lora.at[:, 0:W_LORA].set(w_in[:, c3:c4])
    lora = lora.at[:, LANES:LANES + A_LORA].set(w_in[:, c4:c5])
    lora = lora.at[:, 2 * LANES:2 * LANES + G_LORA].set(w_in[:, c5:c6])
    lora = lora.at[:, F_LANE:F_LANE + heads].set(w_in[:, c6 + c3:c6 + c3 + heads])
    return jnp.concatenate([w_in[:, c6:c6 + c3], w_in[:, :c3]], axis=1).astype(BF16), lora.astype(BF16)


def _pack_mu_lora(mu, width):
    c3 = 3 * width
    c4, c5, c6 = c3 + W_LORA, c3 + W_LORA + A_LORA, c3 + W_LORA + A_LORA + G_LORA
    out = jnp.zeros((LORA_W,), mu.dtype)
    out = out.at[0:W_LORA].set(mu[c3:c4])
    out = out.at[LANES:LANES + A_LORA].set(mu[c4:c5])
    out = out.at[2 * LANES:2 * LANES + G_LORA].set(mu[c5:c6])
    return out[None, :]


def kernel(x, meta_tokens, ffn1_norm, ffn1_w_gu, ffn1_w_down, mix_norm, w_in, rwkv_mu, rwkv_w0,
           rwkv_w_up, rwkv_a0, rwkv_a_up, rwkv_g_up, rwkv_k_k, rwkv_k_a, rwkv_r_k, rwkv_lnx_w,
           rwkv_lnx_b, fox_b_f, fox_out_norm, w_out, ffn2_norm, ffn2_w_gu, ffn2_w_down, final_norm):
    b, seq, d = x.shape
    depth = w_in.shape[0]
    width = rwkv_w0.shape[1]
    heads = fox_b_f.shape[1]
    assert rwkv_g_up.shape[1] == G_LORA and G_LORA <= 2 * LANES
    assert F_LANE >= 2 * LANES + G_LORA and F_LANE + heads <= LORA_W and F_LANE % LANES + heads <= LANES
    l = N_META + seq
    lp = -(-l // TIME_ALIGN) * TIME_ALIGN
    t = b * lp
    assert t % PROJ_TM == 0 and t % FFN_TM == 0 and lp % RWKV_ROWS == 0 and RWKV_ROWS % CHUNK == 0
    assert width % PROJ_TN == 0 and BIAS_PIECES * heads <= LANES

    meta = jnp.broadcast_to(meta_tokens.astype(x.dtype)[None], (b, N_META, d))
    h = jnp.concatenate([meta, x, jnp.zeros((b, lp - l, d), x.dtype)], axis=1).reshape(t, d)

    row = lambda vct: vct.astype(F32)[None, :]
    f_lane0 = F_LANE % LANES
    wgu1, wdn1 = ffn1_w_gu.astype(BF16), ffn1_w_down.astype(BF16)
    wgu2, wdn2 = ffn2_w_gu.astype(BF16), ffn2_w_down.astype(BF16)
    w_out_bf = w_out.astype(BF16)
    for i in range(depth):
        h = _ffn(h, row(ffn1_norm[i]), wgu1, wdn1, i)
        p_main, p_lora = _in_proj(h, row(mix_norm[i]), *_pack_w_in(w_in[i], width, heads),
                                  scaled_blocks=width // PROJ_TN, scale=FOX_HEAD ** -0.5 * LOG2E)
        pm3 = p_main.reshape(b, lp, 6 * width)
        pl3 = p_lora.reshape(b, lp, LORA_W)
        prm = dict(
            mu_rkv=row(rwkv_mu[i, :3 * width]), mu_lora=_pack_mu_lora(rwkv_mu[i], width),
            w0=row(rwkv_w0[i]), w_up=_pad_rows(rwkv_w_up[i], LANES).astype(BF16),
            a0=row(rwkv_a0[i]), a_up=_pad_rows(rwkv_a_up[i], LANES).astype(BF16),
            g_up=_pad_rows(rwkv_g_up[i], 2 * LANES).astype(BF16),
            k_k=row(rwkv_k_k[i]), k_a=row(rwkv_k_a[i]), r_k=row(rwkv_r_k[i].reshape(-1)),
            lnx_w=row(rwkv_lnx_w[i]), lnx_b=row(rwkv_lnx_b[i]),
            b_f=jnp.zeros((1, LANES), F32).at[0, f_lane0:f_lane0 + heads].set(fox_b_f[i]))
        y_rwkv, cp = _rwkv(pm3, pl3, prm, width, heads)
        vt3 = jnp.transpose(pm3[:, :, 2 * width:3 * width], (0, 2, 1))
        y_fox = _attention(pm3, vt3, cp, width)
        h = _out_proj(h, y_rwkv.reshape(t, width), y_fox.reshape(t, width), row(fox_out_norm[i]),
                      w_out_bf, i)
        h = _ffn(h, row(ffn2_norm[i]), wgu2, wdn2, i)
    assert seq % NORM_TM == 0 and NORM_TM % N_META == 0 and N_META % 8 == 0
    return _final_norm(h.reshape(b, lp, d), row(final_norm), seq)
```

```python
import functools
import math

import jax
import jax.numpy as jnp
from jax import lax
from jax.experimental import pallas as pl
from jax.experimental.pallas import tpu as pltpu

F32 = jnp.float32
BF16 = jnp.bfloat16

N_META = 16
RWKV_HEAD = 64
FOX_HEAD = 128
W_LORA = 64
A_LORA = 64
G_LORA = 160
NORM_EPS = 1e-6
LNX_EPS = 64e-5
L2_EPS = 1e-12
NEG_INF = -1e30
LOG2E = math.log2(math.e)
BIAS_PIECES = 3

LANES = 128
VMEM_LIMIT_BYTES = 56 * 1024 * 1024

CHUNK = 64
RWKV_ROWS = 192
PAIR = 2 * RWKV_HEAD
TIME_ALIGN = 384
ATT_T = 384
FFN_TM = 1024
FFN_TF = 512
NORM_TM = 512
PROJ_TM = 1056
PROJ_TN = 1024
OUT_TM = 512
LORA_W = 512
F_LANE = 448

_NT = (((1,), (1,)), ((), ()))


def _rms_scale(x, gain):
    ms = jnp.mean(x * x, axis=-1, keepdims=True)
    return x * lax.rsqrt(ms + NORM_EPS) * gain


def _dot(a, b):
    return jnp.dot(a.astype(BF16), b.astype(BF16), preferred_element_type=F32)


def _dot_nt(a, b):
    return lax.dot_general(a.astype(BF16), b.astype(BF16), _NT, preferred_element_type=F32)


def _split3(x):
    hi = x.astype(BF16)
    r1 = x - hi.astype(F32)
    mid = r1.astype(BF16)
    lo = (r1 - mid.astype(F32)).astype(BF16)
    return hi, mid, lo


def _dot_exact_lhs(m_bf16, x):
    hi, mid, lo = _split3(x)
    acc = jnp.dot(m_bf16, lo, preferred_element_type=F32)
    acc = acc + jnp.dot(m_bf16, mid, preferred_element_type=F32)
    return acc + jnp.dot(m_bf16, hi, preferred_element_type=F32)


def _params(*sem):
    return pltpu.CompilerParams(dimension_semantics=sem, vmem_limit_bytes=VMEM_LIMIT_BYTES)


def _ffn_body(h_ref, g_ref, wg_ref, wu_ref, wd_ref, o_ref, u_ref):
    f = pl.program_id(1)

    @pl.when(f == 0)
    def _():
        x = h_ref[...]
        u_ref[...] = _rms_scale(x, g_ref[...]).astype(BF16)
        o_ref[...] = x

    u = u_ref[...]
    gate = jnp.dot(u, wg_ref[0], preferred_element_type=F32)
    up = jnp.dot(u, wu_ref[0], preferred_element_type=F32)
    act = (0.5 * gate) * jax.nn.sigmoid(gate) * up
    o_ref[...] += jnp.dot(act.astype(BF16), wd_ref[0], preferred_element_type=F32)


def _ffn(h, gain, w_gu, w_down, layer):
    t, d = h.shape
    ff = w_down.shape[1]
    nf = ff // FFN_TF
    return pl.pallas_call(
        _ffn_body,
        grid=(t // FFN_TM, nf),
        in_specs=[
            pl.BlockSpec((FFN_TM, d), lambda i, f: (i, 0)),
            pl.BlockSpec((1, d), lambda i, f: (0, 0)),
            pl.BlockSpec((1, d, FFN_TF), lambda i, f: (layer, 0, f)),
            pl.BlockSpec((1, d, FFN_TF), lambda i, f: (layer, 0, f + nf)),
            pl.BlockSpec((1, FFN_TF, d), lambda i, f: (layer, f, 0)),
        ],
        out_specs=pl.BlockSpec((FFN_TM, d), lambda i, f: (i, 0)),
        out_shape=jax.ShapeDtypeStruct((t, d), F32),
        scratch_shapes=[pltpu.VMEM((FFN_TM, d), BF16)],
        compiler_params=_params("parallel", "arbitrary"),
    )(h, gain, w_gu, w_gu, w_down)


def _proj_body(h_ref, g_ref, w_ref, wl_ref, o_ref, ol_ref, u_ref, *, scaled_blocks, scale):
    j = pl.program_id(1)

    @pl.when(j == 0)
    def _():
        u_ref[...] = _rms_scale(h_ref[...], g_ref[...]).astype(BF16)

    acc = jnp.dot(u_ref[...], w_ref[...], preferred_element_type=F32)
    o_ref[...] = (acc * jnp.where(j < scaled_blocks, scale, 1.0)).astype(o_ref.dtype)

    @pl.when(j == pl.num_programs(1) - 1)
    def _():
        ol_ref[...] = jnp.dot(u_ref[...], wl_ref[...], preferred_element_type=F32)


def _in_proj(h, gain, w, w_lora, scaled_blocks, scale):
    t, d = h.shape
    n = w.shape[1]
    nl = w_lora.shape[1]
    return pl.pallas_call(
        functools.partial(_proj_body, scaled_blocks=scaled_blocks, scale=scale),
        grid=(t // PROJ_TM, n // PROJ_TN),
        in_specs=[
            pl.BlockSpec((PROJ_TM, d), lambda i, j: (i, 0)),
            pl.BlockSpec((1, d), lambda i, j: (0, 0)),
            pl.BlockSpec((d, PROJ_TN), lambda i, j: (0, j)),
            pl.BlockSpec((d, nl), lambda i, j: (0, 0)),
        ],
        out_specs=[
            pl.BlockSpec((PROJ_TM, PROJ_TN), lambda i, j: (i, j)),
            pl.BlockSpec((PROJ_TM, nl), lambda i, j: (i, 0)),
        ],
        out_shape=[jax.ShapeDtypeStruct((t, n), BF16), jax.ShapeDtypeStruct((t, nl), F32)],
        scratch_shapes=[pltpu.VMEM((PROJ_TM, d), BF16)],
        compiler_params=_params("parallel", "arbitrary"),
    )(h, gain, w, w_lora)


def _forget_bias(x, carry_ref, heads, lane0):
    rows = x.shape[0]
    log_f = jnp.minimum(x, 0.0) - jnp.log1p(jnp.exp(-jnp.abs(x)))
    ci = lax.broadcasted_iota(jnp.int32, (rows, rows), 0)
    cj = lax.broadcasted_iota(jnp.int32, (rows, rows), 1)
    csum = _dot_exact_lhs((cj <= ci).astype(BF16), log_f) + carry_ref[...]
    carry_ref[...] = csum[rows - 1:rows, :]
    pieces = _split3(csum * LOG2E)
    src = lax.broadcasted_iota(jnp.int32, (LANES, LANES), 0)
    dst = lax.broadcasted_iota(jnp.int32, (LANES, LANES), 1)
    acc = jnp.zeros((rows, LANES), F32)
    for i, piece in enumerate(pieces):
        move = ((dst % BIAS_PIECES == i) & (dst < BIAS_PIECES * heads)
                & (src == lane0 + dst // BIAS_PIECES)).astype(BF16)
        acc = acc + jnp.dot(piece, move, preferred_element_type=F32)
    return acc.astype(BF16)


def _rwkv_body(rkv_ref, lora_ref, mu_rkv_ref, mu_lora_ref, w0_ref, wup_ref, a0_ref, aup_ref,
               gup_ref, kk_ref, ka_ref, rk_ref, lnw_ref, lnb_ref, bf_ref, o_ref, cp_ref,
               carry_rkv, carry_lora, st_ref, carry_gate, *, width, fox_heads):
    rows = RWKV_ROWS
    n_ch = rows // CHUNK
    c = pl.program_id(1)

    @pl.when(c == 0)
    def _():
        carry_rkv[...] = jnp.zeros_like(carry_rkv)
        carry_lora[...] = jnp.zeros_like(carry_lora)
        st_ref[...] = jnp.zeros_like(st_ref)
        carry_gate[...] = jnp.zeros_like(carry_gate)

    f_blk = slice(F_LANE // LANES * LANES, (F_LANE // LANES + 1) * LANES)
    cp_ref[0] = _forget_bias(lora_ref[0, :, f_blk] + bf_ref[...], carry_gate, fox_heads, F_LANE % LANES)

    row = lax.broadcasted_iota(jnp.int32, (rows, 1), 0)

    def shift(p, carry_ref, mu):
        prev = jnp.where(row == 0, carry_ref[...], pltpu.roll(p, 1, 0))
        carry_ref[...] = p[rows - 1:rows, :]
        return p + (prev - p) * mu

    pf = shift(rkv_ref[0].astype(F32), carry_rkv, mu_rkv_ref[...])
    lf = shift(lora_ref[0], carry_lora, mu_lora_ref[...])
    r = pf[:, :width]
    k = pf[:, width:2 * width]
    v = pf[:, 2 * width:]
    wd = lf[:, 0:LANES]
    ad = lf[:, LANES:2 * LANES]
    gd = lf[:, 2 * LANES:4 * LANES]

    z = w0_ref[...] + _dot(jnp.tanh(wd), wup_ref[...])
    nz = -z
    softplus = jnp.maximum(nz, 0.0) + jnp.log1p(jnp.exp(-jnp.abs(nz)))
    logw = -jnp.exp(-softplus - 0.5)
    a = jax.nn.sigmoid(a0_ref[...] + _dot(ad, aup_ref[...]))
    g = _dot(jax.nn.sigmoid(gd), gup_ref[...])

    ci = lax.broadcasted_iota(jnp.int32, (rows, rows), 0)
    cj = lax.broadcasted_iota(jnp.int32, (rows, rows), 1)
    ltri = ((cj <= ci) & ((ci // CHUNK) == (cj // CHUNK))).astype(BF16)
    cs = _dot_exact_lhs(ltri, logw)

    ii = lax.broadcasted_iota(jnp.int32, (PAIR, PAIR), 0)
    jj = lax.broadcasted_iota(jnp.int32, (PAIR, PAIR), 1)
    same_head = (ii // RWKV_HEAD) == (jj // RWKV_HEAD)
    mask_strict = same_head & (jj < ii)
    mask_incl = same_head & (jj <= ii)
    eye = ii == jj
    seg_ones = same_head.astype(BF16)
    lane = lax.broadcasted_iota(jnp.int32, (CHUNK, PAIR), 1)
    head0 = lane < RWKV_HEAD

    def stack(x):
        return jnp.concatenate([jnp.where(head0, x, 0.0), jnp.where(head0, 0.0, x)], axis=0)

    def both(x):
        return jnp.concatenate([x, x], axis=0)

    pairs = range(width // PAIR)
    sls = [slice(hp * PAIR, (hp + 1) * PAIR) for hp in pairs]
    chunks = [slice(ch * CHUNK, (ch + 1) * CHUNK) for ch in range(n_ch)]
    units = [(ch, p) for ch in range(n_ch) for p in pairs]

    kk = [k[:, sl] * kk_ref[:, sl] for sl in sls]
    ss = [_dot(x * x, seg_ones) for x in kk]
    kk = [x / jnp.maximum(jnp.sqrt(s), L2_EPS) for x, s in zip(kk, ss)]
    k2 = [k[:, sl] * (1.0 + (a[:, sl] - 1.0) * ka_ref[:, sl]) for sl in sls]
    bv = [x * a[:, sl] for x, sl in zip(kk, sls)]
    e_neg = [jnp.exp(-cs[:, sl]) for sl in sls]
    at_ = [-kk[p] * jnp.exp(cs[:, sls[p]] - logw[:, sls[p]]) for p in pairs]
    rt_ = [r[:, sl] * jnp.exp(cs[:, sl]) for sl in sls]
    bt_ = [bv[p] * e_neg[p] for p in pairs]
    kt_ = [k2[p] * e_neg[p] for p in pairs]

    last = [cs[chunks[ch]][CHUNK - 1:CHUNK, sls[p]] for ch, p in units]
    e_end = [jnp.exp(last[u] - cs[chunks[ch], sls[p]]) for u, (ch, p) in enumerate(units)]
    xa = [stack(at_[p][chunks[ch]]) for ch, p in units]
    xr = [stack(rt_[p][chunks[ch]]) for ch, p in units]
    xbh_t = [stack(bv[p][chunks[ch]] * e_end[u]).T for u, (ch, p) in enumerate(units)]
    xkh_t = [stack(k2[p][chunks[ch]] * e_end[u]).T for u, (ch, p) in enumerate(units)]
    vst = [stack(v[chunks[ch], sls[p]]) for ch, p in units]

    big = [_dot_nt(jnp.concatenate([xa[u], xr[u]], axis=0),
                   jnp.concatenate([both(bt_[p][chunks[ch]]), both(kt_[p][chunks[ch]])], axis=0))
           for u, (ch, p) in enumerate(units)]
    big = [x.astype(BF16) for x in big]
    zero = jnp.zeros((PAIR, PAIR), BF16)

    def packed(mask):
        return mask.astype(BF16) != 0

    m_strict, m_incl = packed(mask_strict), packed(mask_incl)
    a_ab = [x[:PAIR, :PAIR] for x in big]
    a_ak = [jnp.where(m_strict, x[:PAIR, PAIR:], zero) for x in big]
    a_rb = [jnp.where(m_incl, x[PAIR:, :PAIR], zero) for x in big]
    a_rk = [jnp.where(m_incl, x[PAIR:, PAIR:], zero) for x in big]

    lvl1 = packed(((ii ^ jj) == 1) & (jj < ii))
    eye_bf = eye.astype(BF16)
    tinv = [eye_bf + jnp.where(lvl1, x, zero) for x in a_ab]
    s = 2
    while s < RWKV_HEAD:
        e_mask = packed(((ii // (2 * s)) == (jj // (2 * s))) & (((ii // s) % 2) == 1)
                        & (((jj // s) % 2) == 0))
        half = [_dot(t_, jnp.where(e_mask, x, zero)).astype(BF16) for t_, x in zip(tinv, a_ab)]
        tinv = [t_ + _dot(h_, t_).astype(BF16) for t_, h_ in zip(tinv, half)]
        s *= 2

    n_u = range(len(units))
    w_av = [_dot(a_ak[u], vst[u]) for u in n_u]
    x = [_dot(tinv[u], jnp.concatenate([xa[u], w_av[u]], axis=1)) for u in n_u]
    a_hat = [x_[:, :PAIR].astype(BF16) for x_ in x]
    u0_v = [jnp.concatenate([x[u][:, PAIR:].astype(BF16), vst[u].astype(BF16)], axis=0) for u in n_u]
    mm = [_dot(xbh_t[u], a_hat[u]) + jnp.where(eye, jnp.exp(last[u]), 0.0) for u in n_u]
    gg = [_dot(jnp.concatenate([xbh_t[u], xkh_t[u]], axis=1), u0_v[u]) for u in n_u]
    rh = [xr[u] + _dot(a_rb[u], a_hat[u]) for u in n_u]
    y0 = [_dot(jnp.concatenate([a_rb[u], a_rk[u]], axis=1), u0_v[u]) for u in n_u]

    st = [st_ref[p] for p in pairs]
    yst = []
    for ch in range(n_ch):
        base = ch * len(pairs)
        yst.append([_dot(rh[base + p], st[p]) + y0[base + p] for p in pairs])
        st = [_dot(mm[base + p], st[p]) + gg[base + p] for p in pairs]
    for p in pairs:
        st_ref[p] = st[p]
    y = [jnp.concatenate([yst[ch][p][:CHUNK] + yst[ch][p][CHUNK:] for ch in range(n_ch)], axis=0)
         for p in pairs]

    inv_n = 1.0 / RWKV_HEAD
    mean = [_dot(x_, seg_ones) * inv_n for x_ in y]
    dlt = [x_ - m_ for x_, m_ in zip(y, mean)]
    var = [_dot(x_ * x_, seg_ones) * inv_n for x_ in dlt]
    bonus = [_dot(r[:, sl] * k2[p] * rk_ref[:, sl], seg_ones) * v[:, sl] for p, sl in zip(pairs, sls)]
    for p, sl in zip(pairs, sls):
        yn = dlt[p] * lax.rsqrt(var[p] + LNX_EPS) * lnw_ref[:, sl] + lnb_ref[:, sl]
        o_ref[0, :, sl] = ((yn + bonus[p]) * g[:, sl]).astype(o_ref.dtype)


def _rwkv(pm3, pl3, prm, width, fox_heads):
    b, lp, _ = pm3.shape
    n_rkv = 3 * width
    vec = lambda n: pl.BlockSpec((1, n), lambda i, c: (0, 0))
    mat = lambda k: pl.BlockSpec((k, width), lambda i, c: (0, 0))
    return pl.pallas_call(
        functools.partial(_rwkv_body, width=width, fox_heads=fox_heads),
        grid=(b, lp // RWKV_ROWS),
        in_specs=[
            pl.BlockSpec((1, RWKV_ROWS, n_rkv), lambda i, c: (i, c, 1)),
            pl.BlockSpec((1, RWKV_ROWS, LORA_W), lambda i, c: (i, c, 0)),
            vec(n_rkv), vec(LORA_W), vec(width), mat(LANES), vec(width), mat(LANES),
            mat(2 * LANES), vec(width), vec(width), vec(width), vec(width), vec(width), vec(LANES),
        ],
        out_specs=[pl.BlockSpec((1, RWKV_ROWS, width), lambda i, c: (i, c, 0)),
                   pl.BlockSpec((1, RWKV_ROWS, LANES), lambda i, c: (i, c, 0))],
        out_shape=[jax.ShapeDtypeStruct((b, lp, width), BF16),
                   jax.ShapeDtypeStruct((b, lp, LANES), BF16)],
        scratch_shapes=[
            pltpu.VMEM((1, n_rkv), F32),
            pltpu.VMEM((1, LORA_W), F32),
            pltpu.VMEM((width // PAIR, PAIR, PAIR), F32),
            pltpu.VMEM((1, LANES), F32),
        ],
        compiler_params=_params("parallel", "arbitrary"),
    )(pm3, pl3, prm["mu_rkv"], prm["mu_lora"], prm["w0"], prm["w_up"], prm["a0"], prm["a_up"],
      prm["g_up"], prm["k_k"], prm["k_a"], prm["r_k"], prm["lnx_w"], prm["lnx_b"], prm["b_f"])


def _attn_body(q_ref, k_ref, vt_ref, cp_ref, o_ref, m_ref, l_ref, acc_ref, *, heads):
    qi = pl.program_id(1)
    m_ref[...] = jnp.full_like(m_ref, NEG_INF)
    l_ref[...] = jnp.zeros_like(l_ref)
    acc_ref[...] = jnp.zeros_like(acc_ref)
    hs = [slice(g * FOX_HEAD, (g + 1) * FOX_HEAD) for g in range(heads)]
    gs = range(heads)
    lane = lax.broadcasted_iota(jnp.int32, (ATT_T, LANES), 1)
    q_aug = [jnp.concatenate(
        [q_ref[0, :, hs[g]], -(lane // BIAS_PIECES == g).astype(BF16)], axis=1) for g in gs]

    def block(j, masked):
        off = pl.multiple_of(j * ATT_T, ATT_T)
        bias = cp_ref[0, pl.ds(off, ATT_T), :]
        s = [lax.dot_general(
            jnp.concatenate([k_ref[0, pl.ds(off, ATT_T), hs[g]], bias], axis=1),
            q_aug[g], _NT, preferred_element_type=F32) for g in gs]
        if masked:
            kpos = lax.broadcasted_iota(jnp.int32, (ATT_T, ATT_T), 0)
            qpos = lax.broadcasted_iota(jnp.int32, (ATT_T, ATT_T), 1)
            s = [jnp.where(kpos <= qpos, x, NEG_INF) for x in s]
        m_prev = [m_ref[g] for g in gs]
        m_new = [jnp.maximum(m_prev[g], jnp.max(s[g], axis=0, keepdims=True)) for g in gs]
        p = [jnp.exp2(s[g] - m_new[g]) for g in gs]
        alpha = [jnp.exp2(m_prev[g] - m_new[g]) for g in gs]
        pv = [jnp.dot(vt_ref[0, hs[g], pl.ds(off, ATT_T)], p[g].astype(BF16),
                      preferred_element_type=F32) for g in gs]
        for g in gs:
            l_ref[g] = alpha[g] * l_ref[g] + jnp.sum(p[g], axis=0, keepdims=True)
            acc_ref[hs[g], :] = alpha[g] * acc_ref[hs[g], :] + pv[g]
            m_ref[g] = m_new[g]

    def body(t, carry):
        block(2 * t, False)
        block(2 * t + 1, False)
        return carry

    lax.fori_loop(0, qi // 2, body, 0)

    @pl.when(qi % 2 == 1)
    def _():
        block(qi - 1, False)

    block(qi, True)
    for g in gs:
        o_ref[0, :, hs[g]] = (acc_ref[hs[g], :] / l_ref[g]).T


def _attention(qkv3, vt3, cp, width):
    b, lp, _ = qkv3.shape
    heads = width // FOX_HEAD
    return pl.pallas_call(
        functools.partial(_attn_body, heads=heads),
        grid=(b, lp // ATT_T),
        in_specs=[
            pl.BlockSpec((1, ATT_T, width), lambda i, q: (i, q, 0)),
            pl.BlockSpec((1, lp, width), lambda i, q: (i, 0, 1)),
            pl.BlockSpec((1, width, lp), lambda i, q: (i, 0, 0)),
            pl.BlockSpec((1, lp, LANES), lambda i, q: (i, 0, 0)),
        ],
        out_specs=pl.BlockSpec((1, ATT_T, width), lambda i, q: (i, q, 0)),
        out_shape=jax.ShapeDtypeStruct((b, lp, width), F32),
        scratch_shapes=[
            pltpu.VMEM((heads, 1, ATT_T), F32),
            pltpu.VMEM((heads, 1, ATT_T), F32),
            pltpu.VMEM((width, ATT_T), F32),
        ],
        compiler_params=_params("parallel", "arbitrary"),
    )(qkv3, qkv3, vt3, cp)


def _out_body(h_ref, yr_ref, yf_ref, gn_ref, w1_ref, w2_ref, o_ref):
    yf = _rms_scale(yf_ref[...], gn_ref[...])
    acc = jnp.dot(yr_ref[...], w1_ref[0], preferred_element_type=F32)
    acc = acc + jnp.dot(yf.astype(BF16), w2_ref[0], preferred_element_type=F32)
    o_ref[...] = h_ref[...] + acc


def _out_proj(h, y_rwkv, y_fox, fox_gain, w_out, layer):
    t, d = h.shape
    w1 = y_rwkv.shape[1]
    w2 = y_fox.shape[1]
    return pl.pallas_call(
        _out_body,
        grid=(t // OUT_TM,),
        in_specs=[
            pl.BlockSpec((OUT_TM, d), lambda i: (i, 0)),
            pl.BlockSpec((OUT_TM, w1), lambda i: (i, 0)),
            pl.BlockSpec((OUT_TM, w2), lambda i: (i, 0)),
            pl.BlockSpec((1, w2), lambda i: (0, 0)),
            pl.BlockSpec((1, w1, d), lambda i: (layer, 0, 0)),
            pl.BlockSpec((1, w2, d), lambda i: (layer, 1, 0)),
        ],
        out_specs=pl.BlockSpec((OUT_TM, d), lambda i: (i, 0)),
        out_shape=jax.ShapeDtypeStruct((t, d), F32),
        compiler_params=_params("parallel"),
    )(h, y_rwkv, y_fox, fox_gain, w_out, w_out)


def _norm_body(a_ref, b_ref, g_ref, o_ref):
    x = jnp.concatenate([a_ref[0, N_META:, :], b_ref[0]], axis=0)
    o_ref[0] = _rms_scale(x, g_ref[...])


def _final_norm(h3, gain, seq):
    b, _, d = h3.shape
    per = NORM_TM // N_META
    return pl.pallas_call(
        _norm_body,
        grid=(b, seq // NORM_TM),
        in_specs=[
            pl.BlockSpec((1, NORM_TM, d), lambda i, r: (i, r, 0)),
            pl.BlockSpec((1, N_META, d), lambda i, r: (i, (r + 1) * per, 0)),
            pl.BlockSpec((1, d), lambda i, r: (0, 0)),
        ],
        out_specs=pl.BlockSpec((1, NORM_TM, d), lambda i, r: (i, r, 0)),
        out_shape=jax.ShapeDtypeStruct((b, seq, d), F32),
        compiler_params=_params("parallel", "parallel"),
    )(h3, h3, gain)


def _pad_rows(w, rows):
    return jnp.pad(w, ((0, rows - w.shape[0]), (0, 0)))


def _pack_w_in(w_in, width, heads):
    d = w_in.shape[0]
    c3 = 3 * width
    c4, c5, c6 = c3 + W_LORA, c3 + W_LORA + A_LORA, c3 + W_LORA + A_LORA + G_LORA
    lora = jnp.zeros((d, LORA_W), w_in.dtype)
    lora = lora.at[:, 0:W_LORA].set(w_in[:, c3:c4])
    lora = lora.at[:, LANES:LANES + A_LORA].set(w_in[:, c4:c5])
    lora = lora.at[:, 2 * LANES:2 * LANES + G_LORA].set(w_in[:, c5:c6])
    lora = lora.at[:, F_LANE:F_LANE + heads].set(w_in[:, c6 + c3:c6 + c3 + heads])
    return jnp.concatenate([w_in[:, c6:c6 + c3], w_in[:, :c3]], axis=1).astype(BF16), lora.astype(BF16)


def _pack_mu_lora(mu, width):
    c3 = 3 * width
    c4, c5, c6 = c3 + W_LORA, c3 + W_LORA + A_LORA, c3 + W_LORA + A_LORA + G_LORA
    out = jnp.zeros((LORA_W,), mu.dtype)
    out = out.at[0:W_LORA].set(mu[c3:c4])
    out = out.at[LANES:LANES + A_LORA].set(mu[c4:c5])
    out = out.at[2 * LANES:2 * LANES + G_LORA].set(mu[c5:c6])
    return out[None, :]


def kernel(x, meta_tokens, ffn1_norm, ffn1_w_gu, ffn1_w_down, mix_norm, w_in, rwkv_mu, rwkv_w0,
           rwkv_w_up, rwkv_a0, rwkv_a_up, rwkv_g_up, rwkv_k_k, rwkv_k_a, rwkv_r_k, rwkv_lnx_w,
           rwkv_lnx_b, fox_b_f, fox_out_norm, w_out, ffn2_norm, ffn2_w_gu, ffn2_w_down, final_norm):
    b, seq, d = x.shape
    depth = w_in.shape[0]
    width = rwkv_w0.shape[1]
    heads = fox_b_f.shape[1]
    assert rwkv_g_up.shape[1] == G_LORA and G_LORA <= 2 * LANES
    assert F_LANE >= 2 * LANES + G_LORA and F_LANE + heads <= LORA_W and F_LANE % LANES + heads <= LANES
    l = N_META + seq
    lp = -(-l // TIME_ALIGN) * TIME_ALIGN
    t = b * lp
    assert t % PROJ_TM == 0 and t % FFN_TM == 0 and lp % RWKV_ROWS == 0 and RWKV_ROWS % CHUNK == 0
    assert width % PROJ_TN == 0 and BIAS_PIECES * heads <= LANES

    meta = jnp.broadcast_to(meta_tokens.astype(x.dtype)[None], (b, N_META, d))
    h = jnp.concatenate([meta, x, jnp.zeros((b, lp - l, d), x.dtype)], axis=1).reshape(t, d)

    row = lambda vct: vct.astype(F32)[None, :]
    f_lane0 = F_LANE % LANES
    wgu1, wdn1 = ffn1_w_gu.astype(BF16), ffn1_w_down.astype(BF16)
    wgu2, wdn2 = ffn2_w_gu.astype(BF16), ffn2_w_down.astype(BF16)
    w_out_bf = w_out.astype(BF16)
    for i in range(depth):
        h = _ffn(h, row(ffn1_norm[i]), wgu1, wdn1, i)
        p_main, p_lora = _in_proj(h, row(mix_norm[i]), *_pack_w_in(w_in[i], width, heads),
                                  scaled_blocks=width // PROJ_TN, scale=FOX_HEAD ** -0.5 * LOG2E)
        pm3 = p_main.reshape(b, lp, 6 * width)
        pl3 = p_lora.reshape(b, lp, LORA_W)
        prm = dict(
            mu_rkv=row(rwkv_mu[i, :3 * width]), mu_lora=_pack_mu_lora(rwkv_mu[i], width),
            w0=row(rwkv_w0[i]), w_up=_pad_rows(rwkv_w_up[i], LANES).astype(BF16),
            a0=row(rwkv_a0[i]), a_up=_pad_rows(rwkv_a_up[i], LANES).astype(BF16),
            g_up=_pad_rows(rwkv_g_up[i], 2 * LANES).astype(BF16),
            k_k=row(rwkv_k_k[i]), k_a=row(rwkv_k_a[i]), r_k=row(rwkv_r_k[i].reshape(-1)),
            lnx_w=row(rwkv_lnx_w[i]), lnx_b=row(rwkv_lnx_b[i]),
            b_f=jnp.zeros((1, LANES), F32).at[0, f_lane0:f_lane0 + heads].set(fox_b_f[i]))
        y_rwkv, cp = _rwkv(pm3, pl3, prm, width, heads)
        vt3 = jnp.transpose(pm3[:, :, 2 * width:3 * width], (0, 2, 1))
        y_fox = _attention(pm3, vt3, cp, width)
        h = _out_proj(h, y_rwkv.reshape(t, width), y_fox.reshape(t, width), row(fox_out_norm[i]),
                      w_out_bf, i)
        h = _ffn(h, row(ffn2_norm[i]), wgu2, wdn2, i)
    assert seq % NORM_TM == 0 and NORM_TM % N_META == 0 and N_META % 8 == 0
    return _final_norm(h.reshape(b, lp, d), row(final_norm), seq)
```
